```python
import math, functools
import jax, jax.numpy as jnp
from jax import lax
import numpy as np

D_MODEL = 1024
BATCH = 16
SEQ = 2048
DEPTH = 2
DEC_BATCH = 32
DEC_SEQ = 1
PAST_LEN = 16384
PAGE_SIZE = 128

N_A_LAYERS = DEPTH // 2
N_B_LAYERS = DEPTH - N_A_LAYERS
H_A = 8
DK_A = D_MODEL // H_A
DV_A = D_MODEL // H_A
D_A = H_A * DK_A
HGRN_CHUNK = 32
HD_B = 64
H_B = D_MODEL // (2 * HD_B)
Q_DIM_B = H_B * 2 * HD_B
V_DIM_B = H_B * 2 * HD_B
Q_BLOCK = 128
N_EXPERTS = 32
TOP_K = 4
D_FF = D_MODEL
SWIGLU_LIMIT = 7.0
SWIGLU_ALPHA = 1.702
MOE_BLOCK = 128
DEEPNORM_ALPHA = (2.0 * DEPTH) ** 0.25
DEEPNORM_BETA = (8.0 * DEPTH) ** -0.25
LN_EPS = 1e-5
NEG_INF = -1e30

kernel_name = 'yoco_hgrn2_diffattn_moe_step'

f32 = jnp.float32


def _layer_norm(x, g, b):
    xf = x.astype(f32)
    mu = jnp.mean(xf, -1, keepdims=True)
    var = jnp.mean(jnp.square(xf - mu), -1, keepdims=True)
    return ((xf - mu) * lax.rsqrt(var + LN_EPS) * g.astype(f32) + b.astype(f32)).astype(x.dtype)


def _rms_norm(x, g):
    xf = x.astype(f32)
    return xf * lax.rsqrt(jnp.mean(xf * xf, -1, keepdims=True) + LN_EPS) * g.astype(f32)


def _lower_bounds(lb_logits):
    p = jax.nn.softmax(lb_logits.astype(f32), axis=0)
    return jnp.cumsum(p, axis=0)[:N_A_LAYERS]


def _hgrn2_chunk(S, q, k, logf, v):
    C = q.shape[2]
    b = jnp.cumsum(logf, axis=2)
    causal = jnp.tril(jnp.ones((C, C), bool))
    rel = jnp.where(causal[:, :, None], b[:, :, :, None, :] - b[:, :, None, :, :], -jnp.inf)
    A = jnp.einsum('bhtd,bhtsd->bhts', q, jnp.exp(rel) * k[:, :, None, :, :])
    o = jnp.einsum('bhts,bhse->bhte', A, v) + jnp.einsum('bhtd,bhde->bhte', q * jnp.exp(b), S)
    b_last = b[:, :, -1:, :]
    S_new = jnp.exp(b_last[:, :, 0, :, None]) * S + jnp.einsum('bhsd,bhse->bhde', k * jnp.exp(b_last - b), v)
    return S_new, o


def _hgrn2_mixer(x, S0, lb, w_in, g_norm, w_o, chunk):
    B, S, _ = x.shape
    q, f_raw, i, g = jnp.split(x @ w_in, 4, axis=-1)
    f_raw = f_raw.astype(f32)
    f = lb + (1.0 - lb) * jax.nn.sigmoid(f_raw)
    k = (1.0 - lb) * jax.nn.sigmoid(-f_raw)
    logf = jnp.log(f)

    def heads(t, d):
        t = t.astype(f32).reshape(B, S // chunk, chunk, H_A, d)
        return t.transpose(1, 0, 3, 2, 4)

    S_fin, o = lax.scan(lambda s, xs: _hgrn2_chunk(s, *xs), S0.astype(f32),
                        (heads(jax.nn.silu(q), DK_A), heads(k, DK_A), heads(logf, DK_A), heads(i, DV_A)))
    o = o.transpose(1, 0, 3, 2, 4).reshape(B, S, D_A)
    o = _rms_norm(o * jax.nn.silu(g.astype(f32)), g_norm)
    return o.astype(x.dtype) @ w_o, S_fin


def _alibi_slopes():
    return jnp.exp2(-8.0 * jnp.arange(1, H_B + 1, dtype=f32) / H_B)


def _attn_partial(q, q_pos, k, v, k_pos, slopes):
    s = jnp.einsum('bqhjd,bkhjd->bhjqk', q.astype(f32), k.astype(f32))
    dist = q_pos[:, None] - k_pos[None, :]
    s = s - slopes[None, :, None, None, None] * dist.astype(f32)[None, None, None]
    s = jnp.where(dist[None, None, None] >= 0, s, NEG_INF)
    m = jnp.max(s, -1)
    p = jnp.exp(s - m[..., None])
    return m, jnp.sum(p, -1), jnp.einsum('bhjqk,bkhe->bhjqe', p, v.astype(f32))


def _merge(a, b):
    m1, l1, acc1 = a
    m2, l2, acc2 = b
    m = jnp.maximum(m1, m2)
    a1 = jnp.exp(m1 - m)
    a2 = jnp.exp(m2 - m)
    return m, l1 * a1 + l2 * a2, acc1 * a1[..., None] + acc2 * a2[..., None]


def _attend_prompt(q, k, v):
    B, S = q.shape[:2]
    nb = S // Q_BLOCK
    pos = jnp.arange(S)
    slopes = _alibi_slopes()
    qb = q.reshape(B, nb, Q_BLOCK, H_B, 2, HD_B).swapaxes(0, 1)

    def block(args):
        qi, i = args
        m, l, acc = _attn_partial(qi, i * Q_BLOCK + jnp.arange(Q_BLOCK), k, v, pos, slopes)
        return acc / l[..., None]

    o = lax.map(block, (qb, jnp.arange(nb)))
    return o.transpose(1, 2, 3, 0, 4, 5).reshape(B, H_B, 2, S, 2 * HD_B)


def _attend_sample(q, k_new, v_new, cache_k, cache_v, page_table):
    B, T = q.shape[:2]
    n_pages = page_table.shape[1]
    q_pos = n_pages * PAGE_SIZE + jnp.arange(T)
    slopes = _alibi_slopes()
    init = (jnp.full((B, H_B, 2, T), NEG_INF, f32), jnp.zeros((B, H_B, 2, T), f32),
            jnp.zeros((B, H_B, 2, T, 2 * HD_B), f32))

    def page_step(carry, xs):
        phys, p = xs
        part = _attn_partial(q, q_pos, cache_k[phys], cache_v[phys], p * PAGE_SIZE + jnp.arange(PAGE_SIZE), slopes)
        return _merge(carry, part), None

    carry, _ = lax.scan(page_step, init, (page_table.T, jnp.arange(n_pages)))
    m, l, acc = _merge(carry, _attn_partial(q, q_pos, k_new, v_new, q_pos, slopes))
    return acc / l[..., None]


def _diff_attn_mixer(x, k_sh, v_sh, attend, w_q, lq1, lk1, lq2, lk2, subln_g, w_o, lam_init):
    B, S, _ = x.shape
    q = (x @ w_q).reshape(B, S, H_B, 2, HD_B) * (HD_B ** -0.5)
    o = attend(q, k_sh, v_sh)
    lam = (jnp.exp(jnp.sum((lq1 * lk1).astype(f32))) - jnp.exp(jnp.sum((lq2 * lk2).astype(f32))) + lam_init)
    d = _rms_norm(o[:, :, 0] - lam * o[:, :, 1], subln_g) * (1.0 - lam_init)
    return d.transpose(0, 2, 1, 3).reshape(B, S, V_DIM_B).astype(x.dtype) @ w_o


def _moe_ffn(x, w_router, b_router, w1, b1, w2, b2):
    shape = x.shape
    xt = x.reshape(-1, D_MODEL)
    T = xt.shape[0]
    logits = jnp.matmul(xt, w_router, preferred_element_type=f32) + b_router.astype(f32)
    top_val, top_idx = lax.top_k(logits, TOP_K)
    gates = jax.nn.softmax(top_val, axis=-1)
    flat_e = top_idx.reshape(-1)
    order = jnp.argsort(flat_e)
    sorted_e = flat_e[order]
    counts = jnp.bincount(flat_e, length=N_EXPERTS)
    padded = (counts + MOE_BLOCK - 1) // MOE_BLOCK * MOE_BLOCK
    pad_end = jnp.cumsum(padded)
    pad_start = pad_end - padded
    start = jnp.cumsum(counts) - counts
    dest = pad_start[sorted_e] + jnp.arange(T * TOP_K) - start[sorted_e]
    tok_sorted = (order // TOP_K).astype(jnp.int32)
    n_blocks = (T * TOP_K + N_EXPERTS * (MOE_BLOCK - 1) + MOE_BLOCK - 1) // MOE_BLOCK
    row_tok = jnp.full((n_blocks * MOE_BLOCK,), T, jnp.int32).at[dest].set(tok_sorted)
    blk_e = jnp.minimum(jnp.searchsorted(pad_end, jnp.arange(n_blocks) * MOE_BLOCK, side='right'), N_EXPERTS - 1)
    x_pad = jnp.concatenate([xt, jnp.zeros((1, D_MODEL), xt.dtype)], axis=0)

    def expert_block(args):
        toks, e = args
        h = jnp.matmul(x_pad[toks], w1[e], preferred_element_type=f32) + b1[e].astype(f32)
        x_glu = jnp.minimum(h[:, :D_FF], SWIGLU_LIMIT)
        x_lin = jnp.clip(h[:, D_FF:], -SWIGLU_LIMIT, SWIGLU_LIMIT)
        act = x_glu * jax.nn.sigmoid(SWIGLU_ALPHA * x_glu) * (x_lin + 1.0)
        return jnp.matmul(act.astype(w2.dtype), w2[e], preferred_element_type=f32) + b2[e].astype(f32)

    y_rows = lax.map(expert_block, (row_tok.reshape(n_blocks, MOE_BLOCK), blk_e)).reshape(-1, D_MODEL)
    w_sorted = gates.reshape(-1)[order]
    y = jax.ops.segment_sum(y_rows[dest] * w_sorted[:, None], tok_sorted, num_segments=T)
    return y.reshape(shape)


def _trunk(x, S0, attend, chunk, lbs, w_in_a, gnorm_a, w_o_a, w_kv, w_q_b, lam_q1, lam_k1, lam_q2, lam_k2,
           subln_b, w_o_b, ln_mix_g, ln_mix_b, ln_ffn_g, ln_ffn_b, w_router, b_router, w1, b1, w2, b2):
    B, S, _ = x.shape
    states = []
    k_sh = v_sh = None
    for layer in range(DEPTH):
        if layer < N_A_LAYERS:
            mix, s_fin = _hgrn2_mixer(x, S0[layer], lbs[layer], w_in_a[layer], gnorm_a[layer], w_o_a[layer], chunk)
            states.append(s_fin)
        else:
            j = layer - N_A_LAYERS
            lam_init = 0.8 - 0.6 * math.exp(-0.3 * layer)
            mix = _diff_attn_mixer(x, k_sh, v_sh, attend, w_q_b[j], lam_q1[j], lam_k1[j], lam_q2[j], lam_k2[j],
                                   subln_b[j], w_o_b[j], lam_init)
        x = _layer_norm(DEEPNORM_ALPHA * x + mix, ln_mix_g[layer], ln_mix_b[layer])
        ffn = _moe_ffn(x, w_router[layer], b_router[layer], w1[layer], b1[layer], w2[layer], b2[layer])
        x = _layer_norm(DEEPNORM_ALPHA * x + ffn, ln_ffn_g[layer], ln_ffn_b[layer])
        if layer == N_A_LAYERS - 1:
            kv = x @ w_kv
            k_sh = kv[..., :Q_DIM_B].reshape(B, S, H_B, 2, HD_B)
            v_sh = kv[..., Q_DIM_B:].reshape(B, S, H_B, 2 * HD_B)
    return x, jnp.stack(states), k_sh, v_sh


def setup_inputs(seed: int = 0) -> dict:
    key = jax.random.key(seed)
    ks = jax.random.split(key, 32)
    n_pages = PAST_LEN // PAGE_SIZE
    n_used = DEC_BATCH * n_pages
    n_pool = n_used + max(1, n_used // 4)
    nrm = jax.random.normal
    d = {}
    d['x_prompt'] = nrm(ks[0], (BATCH, SEQ, D_MODEL), f32)
    d['x_sample'] = nrm(ks[1], (DEC_BATCH, DEC_SEQ, D_MODEL), f32)
    d['state_hgrn'] = 0.3 * nrm(ks[2], (N_A_LAYERS, DEC_BATCH, H_A, DK_A, DV_A), f32)
    d['cache_k'] = nrm(ks[3], (n_pool, PAGE_SIZE, H_B, 2, HD_B), f32)
    d['cache_v'] = 0.5 * nrm(ks[4], (n_pool, PAGE_SIZE, H_B, 2 * HD_B), f32)
    d['page_table'] = jax.random.permutation(ks[5], n_pool)[:n_used].reshape(DEC_BATCH, n_pages).astype(jnp.int32)
    d['w_in_a'] = nrm(ks[6], (N_A_LAYERS, D_MODEL, 4 * D_A), f32) * D_MODEL ** -0.5
    d['lb_logits'] = 0.1 * nrm(ks[7], (N_A_LAYERS + 1, D_A), f32)
    d['gnorm_a'] = 1.0 + 0.02 * nrm(ks[8], (N_A_LAYERS, D_A), f32)
    d['w_o_a'] = nrm(ks[9], (N_A_LAYERS, D_A, D_MODEL), f32) * D_A ** -0.5 * DEEPNORM_BETA
    d['w_kv'] = nrm(ks[10], (D_MODEL, Q_DIM_B + V_DIM_B), f32) * D_MODEL ** -0.5
    d['w_q_b'] = nrm(ks[11], (N_B_LAYERS, D_MODEL, Q_DIM_B), f32) * D_MODEL ** -0.5
    d['lam_q1'] = 0.1 * nrm(ks[12], (N_B_LAYERS, HD_B), f32)
    d['lam_k1'] = 0.1 * nrm(ks[13], (N_B_LAYERS, HD_B), f32)
    d['lam_q2'] = 0.1 * nrm(ks[14], (N_B_LAYERS, HD_B), f32)
    d['lam_k2'] = 0.1 * nrm(ks[15], (N_B_LAYERS, HD_B), f32)
    d['subln_b'] = 1.0 + 0.02 * nrm(ks[16], (N_B_LAYERS, 2 * HD_B), f32)
    d['w_o_b'] = nrm(ks[17], (N_B_LAYERS, V_DIM_B, D_MODEL), f32) * V_DIM_B ** -0.5 * DEEPNORM_BETA
    d['ln_mix_g'] = 1.0 + 0.02 * nrm(ks[18], (DEPTH, D_MODEL), f32)
    d['ln_mix_b'] = 0.02 * nrm(ks[19], (DEPTH, D_MODEL), f32)
    d['ln_ffn_g'] = 1.0 + 0.02 * nrm(ks[20], (DEPTH, D_MODEL), f32)
    d['ln_ffn_b'] = 0.02 * nrm(ks[21], (DEPTH, D_MODEL), f32)
    d['w_router'] = nrm(ks[22], (DEPTH, D_MODEL, N_EXPERTS), f32) * D_MODEL ** -0.5
    d['b_router'] = 0.01 * nrm(ks[23], (DEPTH, N_EXPERTS), f32)
    d['w1'] = nrm(ks[24], (DEPTH, N_EXPERTS, D_MODEL, 2 * D_FF), f32) * D_MODEL ** -0.5
    d['b1'] = 0.01 * nrm(ks[25], (DEPTH, N_EXPERTS, 2 * D_FF), f32)
    d['w2'] = nrm(ks[26], (DEPTH, N_EXPERTS, D_FF, D_MODEL), f32) * D_FF ** -0.5 * DEEPNORM_BETA
    d['b2'] = 0.01 * nrm(ks[27], (DEPTH, N_EXPERTS, D_MODEL), f32)
    return d


def reference(x_prompt, x_sample, state_hgrn, cache_k, cache_v, page_table, w_in_a, lb_logits, gnorm_a, w_o_a,
              w_kv, w_q_b, lam_q1, lam_k1, lam_q2, lam_k2, subln_b, w_o_b, ln_mix_g, ln_mix_b, ln_ffn_g, ln_ffn_b,
              w_router, b_router, w1, b1, w2, b2):
    lbs = _lower_bounds(lb_logits)
    weights = (lbs, w_in_a, gnorm_a, w_o_a, w_kv, w_q_b, lam_q1, lam_k1, lam_q2, lam_k2, subln_b, w_o_b,
               ln_mix_g, ln_mix_b, ln_ffn_g, ln_ffn_b, w_router, b_router, w1, b1, w2, b2)
    S0_prompt = jnp.zeros((N_A_LAYERS, x_prompt.shape[0], H_A, DK_A, DV_A), f32)
    y_prompt, st_prompt, k_prompt, v_prompt = _trunk(x_prompt, S0_prompt, _attend_prompt, HGRN_CHUNK, *weights)
    attend_s = functools.partial(_attend_sample, cache_k=cache_k, cache_v=cache_v, page_table=page_table)
    y_sample, st_sample, k_sample, v_sample = _trunk(x_sample, state_hgrn, attend_s, x_sample.shape[1], *weights)
    return (y_prompt, y_sample, st_prompt, st_sample, k_prompt, v_prompt, k_sample, v_sample)
```

```python
import functools
import math

import numpy as np
import jax
import jax.numpy as jnp
from jax import lax
from jax.experimental import pallas as pl
from jax.experimental.pallas import tpu as pltpu

f32 = jnp.float32
bf16 = jnp.bfloat16

D_MODEL = 1024
DEPTH = 2
N_A_LAYERS = 1
H_A = 8
DK_A = 128
DV_A = 128
D_A = H_A * DK_A
HD_B = 64
H_B = 8
N_EXPERTS = 32
TOP_K = 4
D_FF = D_MODEL
SWIGLU_LIMIT = 7.0
SWIGLU_ALPHA = 1.702
DEEPNORM_ALPHA = (2.0 * DEPTH) ** 0.25
LN_EPS = 1e-5
NEG_INF = -1e30
PAGE_SIZE = 128

LANES = 128
VMEM_LIMIT = 56 * 1024 * 1024

HGRN_TILE = 128
MOE_ROWS = 512


def _cparams(sem):
    return pltpu.CompilerParams(dimension_semantics=sem, vmem_limit_bytes=VMEM_LIMIT)


def _sigmoid(x):
    return 1.0 / (1.0 + jnp.exp(-x))


def _layer_norm_rows(x, g, b):
    mu = jnp.mean(x, -1, keepdims=True)
    xc = x - mu
    var = jnp.mean(xc * xc, -1, keepdims=True)
    return xc * lax.rsqrt(var + LN_EPS) * g + b


def _dot(a, b):
    return jnp.dot(a, b, preferred_element_type=f32)


def _dot_nt(a, b):
    return lax.dot_general(a, b, (((1,), (1,)), ((), ())), preferred_element_type=f32)


def _dot_tn(a, b):
    return lax.dot_general(a, b, (((0,), (0,)), ((), ())), preferred_element_type=f32)


def _inproj_kernel(x_ref, w_ref, lb_ref, qs_ref, k_ref, lf_ref, v_ref, gs_ref):
    h = _dot(x_ref[...].astype(bf16), w_ref[...])
    q = h[:, :D_A]
    fr = h[:, D_A:2 * D_A]
    lb = lb_ref[...]
    z = jnp.exp(-jnp.abs(fr))
    r = 1.0 / (1.0 + z)
    zr = z * r
    pos = fr >= 0
    sig = jnp.where(pos, r, zr)
    nsig = jnp.where(pos, zr, r)
    qs_ref[...] = q * _sigmoid(q)
    k_ref[...] = (1.0 - lb) * nsig
    lf_ref[...] = jnp.log(lb + (1.0 - lb) * sig)
    v_ref[...] = h[:, 2 * D_A:3 * D_A]
    g = h[:, 3 * D_A:]
    gs_ref[...] = g * _sigmoid(g)


def _inproj(x2d, w_in, lb):
    T = x2d.shape[0]
    tm = min(256, T)
    out = jax.ShapeDtypeStruct((T, D_A), f32)
    row = pl.BlockSpec((tm, D_A), lambda i: (i, 0))
    return pl.pallas_call(
        _inproj_kernel,
        grid=(T // tm,),
        in_specs=[pl.BlockSpec((tm, D_MODEL), lambda i: (i, 0)),
                  pl.BlockSpec((D_MODEL, 4 * D_A), lambda i: (0, 0)),
                  pl.BlockSpec((1, D_A), lambda i: (0, 0))],
        out_specs=[row] * 5,
        out_shape=[out] * 5,
        compiler_params=_cparams(("parallel",)),
        name="hgrn_inproj",
    )(x2d, w_in, lb)


def _hgrn_levels(R):
    n = R // 2
    out = []
    while n >= 1:
        out.append(n)
        n //= 2
    return out


def _hgrn_masks(R):
    t = np.arange(R)[:, None]
    s = np.arange(R)[None, :]
    ms = []
    for n in _hgrn_levels(R):
        ms.append((t // (2 * n) == s // (2 * n)) & (t % (2 * n) >= n) & (s % (2 * n) < n))
    ms.append(t == s)
    return np.stack(ms).astype(np.float32)


def _level_ref_rows(b, n):
    R = b.shape[0]
    if 2 * n >= 8:
        b3 = b.reshape(R // (2 * n), 2 * n, LANES)
        return jnp.broadcast_to(b3[:, n - 1:n, :], b3.shape).reshape(R, LANES)
    b3 = b.reshape(R // 8, 8, LANES)
    sub = lax.broadcasted_iota(jnp.int32, b3.shape, 1)
    out = None
    for r0 in range(0, 8, 2 * n):
        row = jnp.broadcast_to(b3[:, r0 + n - 1:r0 + n, :], b3.shape)
        out = row if out is None else jnp.where(sub >= r0, row, out)
    return out.reshape(R, LANES)


def _split3(x):
    hi = x.astype(bf16)
    r1 = x - hi.astype(f32)
    mid = r1.astype(bf16)
    lo = (r1 - mid.astype(f32)).astype(bf16)
    return hi, mid, lo


def _hgrn_seq_kernel(qs_ref, k_ref, lf_ref, v_ref, masks_ref, tril_ref, o_ref, st_ref, state_t):
    R = HGRN_TILE
    levels = _hgrn_levels(R)
    step = pl.program_id(2)

    @pl.when(step == 0)
    def _():
        state_t[...] = jnp.zeros_like(state_t)

    def tile(i, carry):
        r = pl.multiple_of(i * R, R)
        q = qs_ref[pl.ds(r, R), :]
        k = k_ref[pl.ds(r, R), :]
        v = v_ref[pl.ds(r, R), :].astype(bf16)
        hi, mid, lo = _split3(lf_ref[pl.ds(r, R), :])
        c3 = _dot(tril_ref[...], jnp.concatenate([hi, mid, lo], axis=1))
        b = c3[:, :LANES] + c3[:, LANES:2 * LANES] + c3[:, 2 * LANES:]
        a = masks_ref[len(levels)] * _dot_nt(q.astype(bf16), k.astype(bf16))
        for li, n in enumerate(levels):
            ref = _level_ref_rows(b, n)
            qn = q * jnp.exp(jnp.minimum(b - ref, 0.0))
            kn = k * jnp.exp(jnp.minimum(ref - b, 0.0))
            a = a + masks_ref[li] * _dot_nt(qn.astype(bf16), kn.astype(bf16))
        st = state_t[...]
        b_last = b[R - 1:R, :]
        qd = (q * jnp.exp(b)).astype(bf16)
        kd = (k * jnp.exp(b_last - b)).astype(bf16)
        o_ref[pl.ds(r, R), :] = _dot(a.astype(bf16), v) + _dot_nt(qd, st.astype(bf16))
        state_t[...] = st * jnp.exp(b_last) + _dot_tn(v, kd)
        return carry

    lax.fori_loop(0, qs_ref.shape[0] // R, tile, 0)

    @pl.when(step == pl.num_programs(2) - 1)
    def _():
        st_ref[0, 0] = state_t[...]


def _hgrn_seq(qs, k, lf, v, B, S):
    R = HGRN_TILE
    rb = min(512, S)
    nst = S // rb
    masks = jnp.asarray(_hgrn_masks(R))
    tril = jnp.asarray(np.tril(np.ones((R, R), np.float32)), dtype=bf16)
    row = pl.BlockSpec((rb, LANES), lambda b, h, j: (b * nst + j, h))
    o, st_t = pl.pallas_call(
        _hgrn_seq_kernel,
        grid=(B, H_A, nst),
        in_specs=[row, row, row, row,
                  pl.BlockSpec(masks.shape, lambda b, h, j: (0, 0, 0)),
                  pl.BlockSpec((R, R), lambda b, h, j: (0, 0))],
        out_specs=[row, pl.BlockSpec((1, 1, DV_A, DK_A), lambda b, h, j: (b, h, 0, 0))],
        out_shape=[jax.ShapeDtypeStruct((B * S, D_A), f32),
                   jax.ShapeDtypeStruct((B, H_A, DV_A, DK_A), f32)],
        scratch_shapes=[pltpu.VMEM((DV_A, DK_A), f32)],
        compiler_params=_cparams(("parallel", "parallel", "arbitrary")),
        name="hgrn_seq",
    )(qs, k, lf, v, masks, tril)
    return o, jnp.swapaxes(st_t, 2, 3)


def _hgrn_step_kernel(s_ref, q_ref, k_ref, lf_ref, v_ref, o_ref, so_ref):
    for h in range(H_A):
        s_new = jnp.exp(lf_ref[0, h]) * s_ref[0, h] + k_ref[0, h] * v_ref[0, h]
        so_ref[0, h] = s_new
        o_ref[0, h] = jnp.sum(q_ref[0, h] * s_new, axis=0, keepdims=True)


def _hgrn_step(state, qs, k, lf, v):
    B = state.shape[0]
    col = lambda t: t.reshape(B, H_A, DK_A, 1)
    cspec = pl.BlockSpec((1, H_A, DK_A, 1), lambda b: (b, 0, 0, 0))
    rspec = pl.BlockSpec((1, H_A, 1, DV_A), lambda b: (b, 0, 0, 0))
    sspec = pl.BlockSpec((1, H_A, DK_A, DV_A), lambda b: (b, 0, 0, 0))
    o, s_new = pl.pallas_call(
        _hgrn_step_kernel,
        grid=(B,),
        in_specs=[sspec, cspec, cspec, cspec, rspec],
        out_specs=[rspec, sspec],
        out_shape=[jax.ShapeDtypeStruct((B, H_A, 1, DV_A), f32),
                   jax.ShapeDtypeStruct(state.shape, f32)],
        compiler_params=_cparams(("parallel",)),
        name="hgrn_step",
    )(state, col(qs), col(k), col(lf), v.reshape(B, H_A, 1, DV_A))
    return o.reshape(B, D_A), s_new


def _mix_out_kernel(gated, m_ref, gs_ref, gn_ref, x_ref, wo_ref, lg_ref, lbias_ref, wr_ref, br_ref,
                    x1_ref, logit_ref):
    m = m_ref[...].astype(f32)
    if gated:
        m = m * gs_ref[...]
        m = m * lax.rsqrt(jnp.mean(m * m, -1, keepdims=True) + LN_EPS) * gn_ref[...]
    mix = _dot(m.astype(bf16), wo_ref[...])
    x1 = _layer_norm_rows(DEEPNORM_ALPHA * x_ref[...] + mix, lg_ref[...], lbias_ref[...])
    x1_ref[...] = x1
    logit_ref[...] = jnp.dot(x1, wr_ref[...], preferred_element_type=f32,
                             precision=lax.Precision.HIGHEST) + br_ref[...]


def _mix_out(m, gs, gnorm, x2d, w_o, ln_g, ln_b, w_router, b_router, gated):
    T = x2d.shape[0]
    tm = min(256, T)
    row = pl.BlockSpec((tm, D_MODEL), lambda i: (i, 0))
    vec = pl.BlockSpec((1, D_MODEL), lambda i: (0, 0))
    return pl.pallas_call(
        functools.partial(_mix_out_kernel, gated),
        grid=(T // tm,),
        in_specs=[row, row, vec, row,
                  pl.BlockSpec((D_MODEL, D_MODEL), lambda i: (0, 0)), vec, vec,
                  pl.BlockSpec((D_MODEL, N_EXPERTS), lambda i: (0, 0)),
                  pl.BlockSpec((1, N_EXPERTS), lambda i: (0, 0))],
        out_specs=[row, pl.BlockSpec((tm, N_EXPERTS), lambda i: (i, 0))],
        out_shape=[jax.ShapeDtypeStruct((T, D_MODEL), f32),
                   jax.ShapeDtypeStruct((T, N_EXPERTS), f32)],
        compiler_params=_cparams(("parallel",)),
        name="mix_out_gated" if gated else "mix_out",
    )(m, gs, gnorm.reshape(1, -1), x2d, w_o, ln_g.reshape(1, -1), ln_b.reshape(1, -1),
      w_router, b_router.reshape(1, -1))


def _moe_kernel(blk_e_ref, n_used_ref, x_ref, w1_ref, b1_ref, w2_ref, b2_ref, y_ref):
    @pl.when(pl.program_id(0) < n_used_ref[0])
    def _():
        h = _dot(x_ref[...], w1_ref[0]) + b1_ref[0]
        x_glu = jnp.minimum(h[:, :D_FF], SWIGLU_LIMIT)
        x_lin = jnp.clip(h[:, D_FF:], -SWIGLU_LIMIT, SWIGLU_LIMIT)
        act = x_glu * _sigmoid(SWIGLU_ALPHA * x_glu) * (x_lin + 1.0)
        y_ref[...] = _dot(act.astype(bf16), w2_ref[0]) + b2_ref[0]


def _moe_rows(x_rows, blk_e, n_used, w1, b1, w2, b2, bm):
    nb = x_rows.shape[0] // bm
    grid_spec = pltpu.PrefetchScalarGridSpec(
        num_scalar_prefetch=2,
        grid=(nb,),
        in_specs=[pl.BlockSpec((bm, D_MODEL), lambda i, e, n: (i, 0)),
                  pl.BlockSpec((1, D_MODEL, 2 * D_FF), lambda i, e, n: (e[i], 0, 0)),
                  pl.BlockSpec((1, 1, 2 * D_FF), lambda i, e, n: (e[i], 0, 0)),
                  pl.BlockSpec((1, D_FF, D_MODEL), lambda i, e, n: (e[i], 0, 0)),
                  pl.BlockSpec((1, 1, D_MODEL), lambda i, e, n: (e[i], 0, 0))],
        out_specs=pl.BlockSpec((bm, D_MODEL), lambda i, e, n: (i, 0)),
    )
    return pl.pallas_call(
        _moe_kernel,
        grid_spec=grid_spec,
        out_shape=jax.ShapeDtypeStruct((nb * bm, D_MODEL), f32),
        compiler_params=_cparams(("arbitrary",)),
        name="moe_experts",
    )(blk_e, n_used, x_rows, w1, b1.reshape(N_EXPERTS, 1, -1), w2, b2.reshape(N_EXPERTS, 1, -1))


def _moe_ffn(x1, logits, w1, b1, w2, b2, bm):
    T = x1.shape[0]
    top_val, top_idx = lax.top_k(logits, TOP_K)
    gates = jax.nn.softmax(top_val, axis=-1)
    flat_e = top_idx.reshape(-1)
    order = jnp.argsort(flat_e)
    sorted_e = flat_e[order]
    counts = jnp.bincount(flat_e, length=N_EXPERTS)
    padded = (counts + bm - 1) // bm * bm
    pad_end = jnp.cumsum(padded)
    pad_start = pad_end - padded
    start = jnp.cumsum(counts) - counts
    dest = (pad_start[sorted_e] + jnp.arange(T * TOP_K) - start[sorted_e]).astype(jnp.int32)
    tok_sorted = (order // TOP_K).astype(jnp.int32)
    n_blocks = (T * TOP_K + N_EXPERTS * (bm - 1) + bm - 1) // bm
    row_tok = jnp.full((n_blocks * bm,), T, jnp.int32).at[dest].set(tok_sorted)
    blk_e = jnp.minimum(jnp.searchsorted(pad_end, jnp.arange(n_blocks) * bm, side='right'),
                        N_EXPERTS - 1).astype(jnp.int32)
    n_used = (pad_end[-1] // bm).astype(jnp.int32).reshape(1)
    x_pad = jnp.concatenate([x1.astype(bf16), jnp.zeros((1, D_MODEL), bf16)], axis=0)
    y_rows = _moe_rows(x_pad[row_tok], blk_e, n_used, w1, b1, w2, b2, bm)
    pos = jnp.zeros((T * TOP_K,), jnp.int32).at[order].set(dest)
    y = jnp.sum(y_rows[pos].reshape(T, TOP_K, D_MODEL) * gates[:, :, None], axis=1)
    return y


def _ffn_out_kernel(n_proj, x_ref, y_ref, g_ref, b_ref, *rest):
    x2 = _layer_norm_rows(DEEPNORM_ALPHA * x_ref[...] + y_ref[...], g_ref[...], b_ref[...])
    rest[n_proj][...] = x2
    xb = x2.astype(bf16)
    for p in range(n_proj):
        rest[n_proj + 1 + p][...] = _dot(xb, rest[p][...])


def _ffn_out(x1, y, ln_g, ln_b, proj_ws):
    T = x1.shape[0]
    tm = min(256, T)
    row = pl.BlockSpec((tm, D_MODEL), lambda i: (i, 0))
    vec = pl.BlockSpec((1, D_MODEL), lambda i: (0, 0))
    n_proj = len(proj_ws)
    outs = pl.pallas_call(
        functools.partial(_ffn_out_kernel, n_proj),
        grid=(T // tm,),
        in_specs=[row, row, vec, vec] + [pl.BlockSpec(w.shape, lambda i: (0, 0)) for w in proj_ws],
        out_specs=[row] + [pl.BlockSpec((tm, w.shape[1]), lambda i: (i, 0)) for w in proj_ws],
        out_shape=[jax.ShapeDtypeStruct((T, D_MODEL), f32)] +
                  [jax.ShapeDtypeStruct((T, w.shape[1]), f32) for w in proj_ws],
        compiler_params=_cparams(("parallel",)),
        name="ffn_out_%d" % n_proj,
    )(x1, y, ln_g.reshape(1, -1), ln_b.reshape(1, -1), *proj_ws)
    return outs


ATT_TQ = 256


def _attn_seq_kernel(lam_init, slopes_ref, lam_ref, q_ref, k_ref, v_ref, g_ref, o_ref,
                     qs_scr, bias_scr, bias_diag_scr, m_scr, l_scr, acc_scr):
    tq = ATT_TQ
    h = pl.program_id(1)
    i = pl.program_id(2)
    slope = slopes_ref[h]

    @pl.when(i == 0)
    def _():
        row = lax.broadcasted_iota(jnp.int32, (2 * tq, tq), 0)
        col = lax.broadcasted_iota(jnp.int32, (2 * tq, tq), 1)
        row = jnp.where(row >= tq, row - tq, row)
        rel = slope * (col - row).astype(f32)
        bias_scr[...] = rel
        bias_diag_scr[...] = jnp.where(col <= row, rel, NEG_INF)

    q = q_ref[...] * (HD_B ** -0.5)
    lane = lax.broadcasted_iota(jnp.int32, q.shape, 1)
    qs_scr[:tq, :] = jnp.where(lane < HD_B, q, 0.0).astype(bf16)
    qs_scr[tq:, :] = jnp.where(lane >= HD_B, q, 0.0).astype(bf16)
    m_scr[...] = jnp.full_like(m_scr, NEG_INF)
    l_scr[...] = jnp.zeros_like(l_scr)
    acc_scr[...] = jnp.zeros_like(acc_scr)

    def kv_block(j, bias_ref):
        c = pl.multiple_of(j * tq, tq)
        kb = k_ref[pl.ds(c, tq), :].astype(bf16)
        vb = v_ref[pl.ds(c, tq), :].astype(bf16)
        s = _dot_nt(qs_scr[...], kb) + bias_ref[...]
        shift = slope * ((i - j) * tq).astype(f32)
        m_old = m_scr[...]
        m_new = jnp.maximum(m_old, jnp.max(s, -1, keepdims=True) - shift)
        p = jnp.exp(s - (m_new + shift))
        alpha = jnp.exp(m_old - m_new)
        l_scr[...] = alpha * l_scr[...] + jnp.sum(p, -1, keepdims=True)
        acc_scr[...] = alpha * acc_scr[...] + _dot(p.astype(bf16), vb)
        m_scr[...] = m_new

    def body(j, carry):
        kv_block(j, bias_scr)
        return carry

    lax.fori_loop(0, i, body, 0)
    kv_block(i, bias_diag_scr)

    o = acc_scr[...] / l_scr[...]
    d = o[:tq, :] - lam_ref[0] * o[tq:, :]
    d = d * lax.rsqrt(jnp.mean(d * d, -1, keepdims=True) + LN_EPS) * g_ref[...]
    o_ref[...] = (d * (1.0 - lam_init)).astype(o_ref.dtype)


def _attn_seq(q, k, v, lam, subln, lam_init, B, S):
    tq = ATT_TQ
    nq = S // tq
    slopes = jnp.exp2(-8.0 * jnp.arange(1, H_B + 1, dtype=f32) / H_B)
    smem = pl.BlockSpec(memory_space=pltpu.SMEM)
    kv = pl.BlockSpec((S, LANES), lambda b, h, i: (b, h))
    return pl.pallas_call(
        functools.partial(_attn_seq_kernel, lam_init),
        grid=(B, H_B, nq),
        in_specs=[smem, smem,
                  pl.BlockSpec((tq, LANES), lambda b, h, i: (b * nq + i, h)), kv, kv,
                  pl.BlockSpec((1, LANES), lambda b, h, i: (0, 0))],
        out_specs=pl.BlockSpec((tq, LANES), lambda b, h, i: (b * nq + i, h)),
        out_shape=jax.ShapeDtypeStruct((B * S, D_MODEL), bf16),
        scratch_shapes=[pltpu.VMEM((2 * tq, LANES), bf16),
                        pltpu.VMEM((2 * tq, tq), f32),
                        pltpu.VMEM((2 * tq, tq), f32),
                        pltpu.VMEM((2 * tq, 1), f32),
                        pltpu.VMEM((2 * tq, 1), f32),
                        pltpu.VMEM((2 * tq, LANES), f32)],
        compiler_params=_cparams(("parallel", "arbitrary", "arbitrary")),
        name="attn_seq",
    )(slopes, lam.reshape(1), q, k, v, subln.reshape(1, -1))


def _attn_paged_kernel(lam_init, n_pages, pt_ref, slopes_ref, lam_ref, q_ref, kn_ref, vn_ref, g_ref,
                       kc_ref, vc_ref, o_ref, qbd_scr, m_scr, l_scr, acc_scr):
    p_idx = pl.program_id(1)
    nrow = 2 * H_B
    rowi = lax.broadcasted_iota(jnp.int32, (nrow, D_MODEL), 0)
    coli = lax.broadcasted_iota(jnp.int32, (nrow, D_MODEL), 1)
    own = (coli // HD_B) == rowi
    slope_col = jnp.zeros((nrow, 1), f32)
    r1 = lax.broadcasted_iota(jnp.int32, (nrow, 1), 0)
    for hh in range(H_B):
        slope_col = jnp.where(r1 // 2 == hh, slopes_ref[hh], slope_col)

    @pl.when(p_idx == 0)
    def _():
        q = q_ref[0] * (HD_B ** -0.5)
        qbd = jnp.where(own, jnp.broadcast_to(q, (nrow, D_MODEL)), 0.0)
        qbd_scr[...] = qbd.astype(bf16)
        s_new = jnp.sum(qbd_scr[...].astype(f32) * kn_ref[0].astype(bf16).astype(f32), -1, keepdims=True)
        m_scr[...] = s_new
        l_scr[...] = jnp.ones_like(l_scr)
        acc_scr[...] = jnp.broadcast_to(vn_ref[0].astype(bf16).astype(f32), (nrow, D_MODEL))

    kb = kc_ref[0].astype(bf16)
    vb = vc_ref[0].astype(bf16)
    tok = lax.broadcasted_iota(jnp.int32, (nrow, PAGE_SIZE), 1)
    dist = (n_pages * PAGE_SIZE - p_idx * PAGE_SIZE - tok).astype(f32)
    s = _dot_nt(qbd_scr[...], kb) - slope_col * dist
    m_old = m_scr[...]
    m_new = jnp.maximum(m_old, jnp.max(s, -1, keepdims=True))
    p = jnp.exp(s - m_new)
    alpha = jnp.exp(m_old - m_new)
    l_scr[...] = alpha * l_scr[...] + jnp.sum(p, -1, keepdims=True)
    acc_scr[...] = alpha * acc_scr[...] + _dot(p.astype(bf16), vb)
    m_scr[...] = m_new

    @pl.when(p_idx == n_pages - 1)
    def _():
        o = acc_scr[...] / l_scr[...]
        for hh in range(H_B):
            blk = o[2 * hh:2 * hh + 2, hh * LANES:(hh + 1) * LANES]
            d = blk[0:1, :] - lam_ref[0] * blk[1:2, :]
            d = d * lax.rsqrt(jnp.mean(d * d, -1, keepdims=True) + LN_EPS) * g_ref[...]
            o_ref[0, :, hh * LANES:(hh + 1) * LANES] = (d * (1.0 - lam_init)).astype(o_ref.dtype)


def _attn_paged(q, k_new, v_new, cache_k, cache_v, page_table, lam, subln, lam_init):
    B = q.shape[0]
    n_pages = page_table.shape[1]
    slopes = jnp.exp2(-8.0 * jnp.arange(1, H_B + 1, dtype=f32) / H_B)
    smem = pl.BlockSpec(memory_space=pltpu.SMEM)
    vec = pl.BlockSpec((1, 1, D_MODEL), lambda b, p, pt: (b, 0, 0))
    page = pl.BlockSpec((1, PAGE_SIZE, D_MODEL), lambda b, p, pt: (pt[b * n_pages + p], 0, 0))
    grid_spec = pltpu.PrefetchScalarGridSpec(
        num_scalar_prefetch=1,
        grid=(B, n_pages),
        in_specs=[smem, smem, vec, vec, vec,
                  pl.BlockSpec((1, LANES), lambda b, p, pt: (0, 0)), page, page],
        out_specs=vec,
        scratch_shapes=[pltpu.VMEM((2 * H_B, D_MODEL), bf16),
                        pltpu.VMEM((2 * H_B, 1), f32),
                        pltpu.VMEM((2 * H_B, 1), f32),
                        pltpu.VMEM((2 * H_B, D_MODEL), f32)],
    )
    d = pl.pallas_call(
        functools.partial(_attn_paged_kernel, lam_init, n_pages),
        grid_spec=grid_spec,
        out_shape=jax.ShapeDtypeStruct((B, 1, D_MODEL), bf16),
        compiler_params=_cparams(("parallel", "arbitrary")),
        name="attn_paged",
    )(page_table.reshape(-1), slopes, lam.reshape(1), q.reshape(B, 1, -1), k_new.reshape(B, 1, -1),
      v_new.reshape(B, 1, -1), subln.reshape(1, -1), cache_k, cache_v)
    return d.reshape(B, D_MODEL)


def _trunk(x, hgrn, attend, W, moe_rows):
    B, S, _ = x.shape
    T = B * S
    x2d = x.reshape(T, D_MODEL)
    qs, k, lf, v, gs = _inproj(x2d, W['w_in'], W['lb'])
    o, state = hgrn(qs, k, lf, v)
    x1, logits = _mix_out(o, gs, W['gnorm'], x2d, W['w_o_a'], W['ln_mix_g'][0], W['ln_mix_b'][0],
                          W['w_router'][0], W['b_router'][0], gated=True)
    y = _moe_ffn(x1, logits, W['w1'][0], W['b1'][0], W['w2'][0], W['b2'][0], moe_rows)
    x2, k_sh, v_sh, q = _ffn_out(x1, y, W['ln_ffn_g'][0], W['ln_ffn_b'][0],
                                 [W['w_k'], W['w_v'], W['w_q']])
    d = attend(q, k_sh, v_sh)
    x3, logits = _mix_out(d, d, W['gnorm'], x2, W['w_o_b'], W['ln_mix_g'][1], W['ln_mix_b'][1],
                          W['w_router'][1], W['b_router'][1], gated=False)
    y = _moe_ffn(x3, logits, W['w1'][1], W['b1'][1], W['w2'][1], W['b2'][1], moe_rows)
    (x4,) = _ffn_out(x3, y, W['ln_ffn_g'][1], W['ln_ffn_b'][1], [])
    return (x4.reshape(B, S, D_MODEL), state[None],
            k_sh.reshape(B, S, H_B, 2, HD_B), v_sh.reshape(B, S, H_B, 2 * HD_B))


def kernel(x_prompt, x_sample, state_hgrn, cache_k, cache_v, page_table, w_in_a, lb_logits, gnorm_a, w_o_a,
           w_kv, w_q_b, lam_q1, lam_k1, lam_q2, lam_k2, subln_b, w_o_b, ln_mix_g, ln_mix_b, ln_ffn_g, ln_ffn_b,
           w_router, b_router, w1, b1, w2, b2):
    q_dim = H_B * 2 * HD_B
    lb = jnp.cumsum(jax.nn.softmax(lb_logits.astype(f32), axis=0), axis=0)[:N_A_LAYERS]
    W = dict(
        lb=lb[0].reshape(1, D_A), w_in=w_in_a[0].astype(bf16), gnorm=gnorm_a[0], w_o_a=w_o_a[0].astype(bf16),
        w_k=w_kv[:, :q_dim].astype(bf16), w_v=w_kv[:, q_dim:].astype(bf16), w_q=w_q_b[0].astype(bf16),
        w_o_b=w_o_b[0].astype(bf16), ln_mix_g=ln_mix_g, ln_mix_b=ln_mix_b, ln_ffn_g=ln_ffn_g, ln_ffn_b=ln_ffn_b,
        w_router=w_router, b_router=b_router, w1=w1.astype(bf16), b1=b1, w2=w2.astype(bf16), b2=b2)
    layer = N_A_LAYERS
    lam_init = 0.8 - 0.6 * math.exp(-0.3 * layer)
    lam = (jnp.exp(jnp.sum((lam_q1[0] * lam_k1[0]).astype(f32)))
           - jnp.exp(jnp.sum((lam_q2[0] * lam_k2[0]).astype(f32))) + lam_init)
    subln = subln_b[0]

    Bp, Sp, _ = x_prompt.shape
    y_p, st_p, k_p, v_p = _trunk(
        x_prompt,
        lambda qs, k, lf, v: _hgrn_seq(qs, k, lf, v, Bp, Sp),
        lambda q, k_sh, v_sh: _attn_seq(q, k_sh, v_sh, lam, subln, lam_init, Bp, Sp),
        W, MOE_ROWS)

    n_pool = cache_k.shape[0]
    ck = cache_k.reshape(n_pool, PAGE_SIZE, D_MODEL)
    cv = cache_v.reshape(n_pool, PAGE_SIZE, D_MODEL)
    y_s, st_s, k_s, v_s = _trunk(
        x_sample,
        lambda qs, k, lf, v: _hgrn_step(state_hgrn[0], qs, k, lf, v),
        lambda q, k_sh, v_sh: _attn_paged(q, k_sh, v_sh, ck, cv, page_table, lam, subln, lam_init),
        W, 32)
    return (y_p, y_s, st_p, st_s, k_p, v_p, k_s, v_s)
```

```python
import functools
import math

import numpy as np
import jax
import jax.numpy as jnp
from jax import lax
from jax.experimental import pallas as pl
from jax.experimental.pallas import tpu as pltpu

f32 = jnp.float32
bf16 = jnp.bfloat16

D_MODEL = 1024
DEPTH = 2
N_A_LAYERS = 1
H_A = 8
DK_A = 128
DV_A = 128
D_A = H_A * DK_A
HD_B = 64
H_B = 8
N_EXPERTS = 32
TOP_K = 4
D_FF = D_MODEL
SWIGLU_LIMIT = 7.0
SWIGLU_ALPHA = 1.702
DEEPNORM_ALPHA = (2.0 * DEPTH) ** 0.25
LN_EPS = 1e-5
NEG_INF = -1e30
PAGE_SIZE = 128

LANES = 128
VMEM_LIMIT = 56 * 1024 * 1024
MOE_VMEM_LIMIT = 60 * 1024 * 1024

HGRN_TILE = 128
MOE_ROWS = 512


def _cparams(sem):
    return pltpu.CompilerParams(dimension_semantics=sem, vmem_limit_bytes=VMEM_LIMIT)


def _sigmoid(x):
    return 1.0 / (1.0 + jnp.exp(-x))


def _layer_norm_rows(x, g, b):
    mu = jnp.mean(x, -1, keepdims=True)
    xc = x - mu
    var = jnp.mean(xc * xc, -1, keepdims=True)
    return xc * lax.rsqrt(var + LN_EPS) * g + b


def _dot(a, b):
    return jnp.dot(a, b, preferred_element_type=f32)


def _dot_nt(a, b):
    return lax.dot_general(a, b, (((1,), (1,)), ((), ())), preferred_element_type=f32)


def _dot_tn(a, b):
    return lax.dot_general(a, b, (((0,), (0,)), ((), ())), preferred_element_type=f32)


def _inproj_kernel(x_ref, w_ref, lb_ref, qs_ref, k_ref, lf_ref, v_ref, gs_ref):
    h = _dot(x_ref[...].astype(bf16), w_ref[...])
    q = h[:, :D_A]
    fr = h[:, D_A:2 * D_A]
    lb = lb_ref[...]
    z = jnp.exp(-jnp.abs(fr))
    r = 1.0 / (1.0 + z)
    zr = z * r
    pos = fr >= 0
    sig = jnp.where(pos, r, zr)
    nsig = jnp.where(pos, zr, r)
    qs_ref[...] = q * _sigmoid(q)
    k_ref[...] = (1.0 - lb) * nsig
    lf_ref[...] = jnp.log(lb + (1.0 - lb) * sig)
    v_ref[...] = h[:, 2 * D_A:3 * D_A]
    g = h[:, 3 * D_A:]
    gs_ref[...] = g * _sigmoid(g)


def _inproj(x2d, w_in, lb):
    T = x2d.shape[0]
    tm = min(256, T)
    out = jax.ShapeDtypeStruct((T, D_A), f32)
    row = pl.BlockSpec((tm, D_A), lambda i: (i, 0))
    return pl.pallas_call(
        _inproj_kernel,
        grid=(T // tm,),
        in_specs=[pl.BlockSpec((tm, D_MODEL), lambda i: (i, 0)),
                  pl.BlockSpec((D_MODEL, 4 * D_A), lambda i: (0, 0)),
                  pl.BlockSpec((1, D_A), lambda i: (0, 0))],
        out_specs=[row] * 5,
        out_shape=[out] * 5,
        compiler_params=_cparams(("parallel",)),
        name="hgrn_inproj",
    )(x2d, w_in, lb)


def _hgrn_levels(R):
    n = R // 2
    out = []
    while n >= 1:
        out.append(n)
        n //= 2
    return out


def _hgrn_masks(R):
    t = np.arange(R)[:, None]
    s = np.arange(R)[None, :]
    ms = []
    for n in _hgrn_levels(R):
        ms.append((t // (2 * n) == s // (2 * n)) & (t % (2 * n) >= n) & (s % (2 * n) < n))
    ms.append(t == s)
    return np.stack(ms).astype(np.float32)


def _level_ref_rows(b, n):
    R = b.shape[0]
    if 2 * n >= 8:
        b3 = b.reshape(R // (2 * n), 2 * n, LANES)
        return jnp.broadcast_to(b3[:, n - 1:n, :], b3.shape).reshape(R, LANES)
    b3 = b.reshape(R // 8, 8, LANES)
    sub = lax.broadcasted_iota(jnp.int32, b3.shape, 1)
    out = None
    for r0 in range(0, 8, 2 * n):
        row = jnp.broadcast_to(b3[:, r0 + n - 1:r0 + n, :], b3.shape)
        out = row if out is None else jnp.where(sub >= r0, row, out)
    return out.reshape(R, LANES)


def _split3(x):
    hi = x.astype(bf16)
    r1 = x - hi.astype(f32)
    mid = r1.astype(bf16)
    lo = (r1 - mid.astype(f32)).astype(bf16)
    return hi, mid, lo


def _hgrn_seq_kernel(qs_ref, k_ref, lf_ref, v_ref, masks_ref, tril_ref, o_ref, st_ref, state_t):
    R = HGRN_TILE
    levels = _hgrn_levels(R)
    step = pl.program_id(2)

    @pl.when(step == 0)
    def _():
        state_t[...] = jnp.zeros_like(state_t)

    def tile(i, carry):
        r = pl.multiple_of(i * R, R)
        q = qs_ref[pl.ds(r, R), :]
        k = k_ref[pl.ds(r, R), :]
        v = v_ref[pl.ds(r, R), :].astype(bf16)
        hi, mid, lo = _split3(lf_ref[pl.ds(r, R), :])
        c3 = _dot(tril_ref[...], jnp.concatenate([hi, mid, lo], axis=1))
        b = c3[:, :LANES] + c3[:, LANES:2 * LANES] + c3[:, 2 * LANES:]
        a = masks_ref[len(levels)] * _dot_nt(q.astype(bf16), k.astype(bf16))
        for li, n in enumerate(levels):
            ref = _level_ref_rows(b, n)
            qn = q * jnp.exp(jnp.minimum(b - ref, 0.0))
            kn = k * jnp.exp(jnp.minimum(ref - b, 0.0))
            a = a + masks_ref[li] * _dot_nt(qn.astype(bf16), kn.astype(bf16))
        st = state_t[...]
        b_last = b[R - 1:R, :]
        qd = (q * jnp.exp(b)).astype(bf16)
        kd = (k * jnp.exp(b_last - b)).astype(bf16)
        o_ref[pl.ds(r, R), :] = _dot(a.astype(bf16), v) + _dot_nt(qd, st.astype(bf16))
        state_t[...] = st * jnp.exp(b_last) + _dot_tn(v, kd)
        return carry

    lax.fori_loop(0, qs_ref.shape[0] // R, tile, 0)

    @pl.when(step == pl.num_programs(2) - 1)
    def _():
        st_ref[0, 0] = state_t[...]


def _hgrn_seq(qs, k, lf, v, B, S):
    R = HGRN_TILE
    rb = min(512, S)
    nst = S // rb
    masks = jnp.asarray(_hgrn_masks(R))
    tril = jnp.asarray(np.tril(np.ones((R, R), np.float32)), dtype=bf16)
    row = pl.BlockSpec((rb, LANES), lambda b, h, j: (b * nst + j, h))
    o, st_t = pl.pallas_call(
        _hgrn_seq_kernel,
        grid=(B, H_A, nst),
        in_specs=[row, row, row, row,
                  pl.BlockSpec(masks.shape, lambda b, h, j: (0, 0, 0)),
                  pl.BlockSpec((R, R), lambda b, h, j: (0, 0))],
        out_specs=[row, pl.BlockSpec((1, 1, DV_A, DK_A), lambda b, h, j: (b, h, 0, 0))],
        out_shape=[jax.ShapeDtypeStruct((B * S, D_A), f32),
                   jax.ShapeDtypeStruct((B, H_A, DV_A, DK_A), f32)],
        scratch_shapes=[pltpu.VMEM((DV_A, DK_A), f32)],
        compiler_params=_cparams(("parallel", "parallel", "arbitrary")),
        name="hgrn_seq",
    )(qs, k, lf, v, masks, tril)
    return o, jnp.swapaxes(st_t, 2, 3)


def _hgrn_step_kernel(s_ref, q_ref, k_ref, lf_ref, v_ref, o_ref, so_ref):
    for h in range(H_A):
        s_new = jnp.exp(lf_ref[0, h]) * s_ref[0, h] + k_ref[0, h] * v_ref[0, h]
        so_ref[0, h] = s_new
        o_ref[0, h] = jnp.sum(q_ref[0, h] * s_new, axis=0, keepdims=True)


def _hgrn_step(state, qs, k, lf, v):
    B = state.shape[0]
    col = lambda t: t.reshape(B, H_A, DK_A, 1)
    cspec = pl.BlockSpec((1, H_A, DK_A, 1), lambda b: (b, 0, 0, 0))
    rspec = pl.BlockSpec((1, H_A, 1, DV_A), lambda b: (b, 0, 0, 0))
    sspec = pl.BlockSpec((1, H_A, DK_A, DV_A), lambda b: (b, 0, 0, 0))
    o, s_new = pl.pallas_call(
        _hgrn_step_kernel,
        grid=(B,),
        in_specs=[sspec, cspec, cspec, cspec, rspec],
        out_specs=[rspec, sspec],
        out_shape=[jax.ShapeDtypeStruct((B, H_A, 1, DV_A), f32),
                   jax.ShapeDtypeStruct(state.shape, f32)],
        compiler_params=_cparams(("parallel",)),
        name="hgrn_step",
    )(state, col(qs), col(k), col(lf), v.reshape(B, H_A, 1, DV_A))
    return o.reshape(B, D_A), s_new


def _mix_out_kernel(gated, *refs):
    if gated:
        m_ref, gs_ref, gn_ref, x_ref, wo_ref, lg_ref, lbias_ref, wr_ref, br_ref, x1_ref, logit_ref = refs
        m = m_ref[...] * gs_ref[...]
        m = m * lax.rsqrt(jnp.mean(m * m, -1, keepdims=True) + LN_EPS) * gn_ref[...]
        m = m.astype(bf16)
    else:
        m_ref, x_ref, wo_ref, lg_ref, lbias_ref, wr_ref, br_ref, x1_ref, logit_ref = refs
        m = m_ref[...]
    mix = _dot(m, wo_ref[...])
    x1 = _layer_norm_rows(DEEPNORM_ALPHA * x_ref[...] + mix, lg_ref[...], lbias_ref[...])
    x1_ref[...] = x1
    logit_ref[...] = jnp.dot(x1, wr_ref[...], preferred_element_type=f32,
                             precision=lax.Precision.HIGHEST) + br_ref[...]


def _mix_out(m, gate, x2d, w_o, ln_g, ln_b, w_router, b_router):
    T = x2d.shape[0]
    tm = min(256, T)
    row = pl.BlockSpec((tm, D_MODEL), lambda i: (i, 0))
    vec = pl.BlockSpec((1, D_MODEL), lambda i: (0, 0))
    gated = gate is not None
    ins = [m] + ([gate[0], gate[1].reshape(1, -1)] if gated else []) + [
        x2d, w_o, ln_g.reshape(1, -1), ln_b.reshape(1, -1), w_router, b_router.reshape(1, -1)]
    specs = [row] + ([row, vec] if gated else []) + [
        row, pl.BlockSpec((D_MODEL, D_MODEL), lambda i: (0, 0)), vec, vec,
        pl.BlockSpec((D_MODEL, N_EXPERTS), lambda i: (0, 0)),
        pl.BlockSpec((1, N_EXPERTS), lambda i: (0, 0))]
    return pl.pallas_call(
        functools.partial(_mix_out_kernel, gated),
        grid=(T // tm,),
        in_specs=specs,
        out_specs=[row, pl.BlockSpec((tm, N_EXPERTS), lambda i: (i, 0))],
        out_shape=[jax.ShapeDtypeStruct((T, D_MODEL), f32),
                   jax.ShapeDtypeStruct((T, N_EXPERTS), f32)],
        compiler_params=_cparams(("parallel",)),
        name="mix_out_gated" if gated else "mix_out",
    )(*ins)


MOE_FF_CHUNK = 512


def _moe_kernel(blk_e_ref, n_used_ref, x_ref, w1_ref, b1_ref, w2_ref, b2_ref, y_ref, w1b, w2b):
    i = pl.program_id(0)
    live = i < n_used_ref[0]
    prev = blk_e_ref[jnp.maximum(i - 1, 0)]

    @pl.when(live & ((i == 0) | (blk_e_ref[i] != prev)))
    def _():
        w1b[...] = w1_ref[0, 0].astype(bf16)
        w2b[...] = w2_ref[0, 0].astype(bf16)

    @pl.when(live)
    def _():
        x = x_ref[...]
        y = None
        for c in range(0, D_FF, MOE_FF_CHUNK):
            hg = _dot(x, w1b[:, c:c + MOE_FF_CHUNK]) + b1_ref[0, 0, :, c:c + MOE_FF_CHUNK]
            hl = (_dot(x, w1b[:, D_FF + c:D_FF + c + MOE_FF_CHUNK])
                  + b1_ref[0, 0, :, D_FF + c:D_FF + c + MOE_FF_CHUNK])
            x_glu = jnp.minimum(hg, SWIGLU_LIMIT)
            x_lin = jnp.clip(hl, -SWIGLU_LIMIT, SWIGLU_LIMIT)
            act = x_glu * _sigmoid(SWIGLU_ALPHA * x_glu) * (x_lin + 1.0)
            part = _dot(act.astype(bf16), w2b[c:c + MOE_FF_CHUNK, :])
            y = part if y is None else y + part
        y_ref[...] = y + b2_ref[0, 0]


def _moe_rows(x_rows, blk_e, n_used, layer, w1, b1, w2, b2, bm):
    nb = x_rows.shape[0] // bm
    grid_spec = pltpu.PrefetchScalarGridSpec(
        num_scalar_prefetch=2,
        grid=(nb,),
        in_specs=[pl.BlockSpec((bm, D_MODEL), lambda i, e, n: (i, 0)),
                  pl.BlockSpec((1, 1, D_MODEL, 2 * D_FF), lambda i, e, n: (layer, e[i], 0, 0)),
                  pl.BlockSpec((1, 1, 1, 2 * D_FF), lambda i, e, n: (layer, e[i], 0, 0)),
                  pl.BlockSpec((1, 1, D_FF, D_MODEL), lambda i, e, n: (layer, e[i], 0, 0)),
                  pl.BlockSpec((1, 1, 1, D_MODEL), lambda i, e, n: (layer, e[i], 0, 0))],
        out_specs=pl.BlockSpec((bm, D_MODEL), lambda i, e, n: (i, 0)),
        scratch_shapes=[pltpu.VMEM((D_MODEL, 2 * D_FF), bf16), pltpu.VMEM((D_FF, D_MODEL), bf16)],
    )
    return pl.pallas_call(
        _moe_kernel,
        grid_spec=grid_spec,
        out_shape=jax.ShapeDtypeStruct((nb * bm, D_MODEL), f32),
        compiler_params=pltpu.CompilerParams(dimension_semantics=("arbitrary",),
                                             vmem_limit_bytes=MOE_VMEM_LIMIT),
        name="moe_experts",
    )(blk_e, n_used, x_rows, w1, b1.reshape(DEPTH, N_EXPERTS, 1, -1), w2, b2.reshape(DEPTH, N_EXPERTS, 1, -1))


def _moe_ffn(x1, logits, layer, w1, b1, w2, b2, bm):
    T = x1.shape[0]
    n_slots = T * TOP_K
    top_val, top_idx = lax.top_k(logits, TOP_K)
    gates = jax.nn.softmax(top_val, axis=-1)
    flat_e = top_idx.reshape(-1)
    order = jnp.argsort(flat_e)
    inv = jnp.argsort(order)
    counts = jnp.bincount(flat_e, length=N_EXPERTS)
    padded = (counts + bm - 1) // bm * bm
    pad_end = jnp.cumsum(padded)
    pad_start = pad_end - padded
    start = jnp.cumsum(counts) - counts
    n_blocks = (n_slots + N_EXPERTS * (bm - 1) + bm - 1) // bm
    blk_e = jnp.minimum(jnp.searchsorted(pad_end, jnp.arange(n_blocks) * bm, side='right'),
                        N_EXPERTS - 1).astype(jnp.int32)
    n_used = (pad_end[-1] // bm).astype(jnp.int32).reshape(1)
    row = jnp.arange(n_blocks * bm)
    row_e = jnp.repeat(blk_e, bm)
    j = row - pad_start[row_e]
    src = jnp.clip(start[row_e] + j, 0, n_slots - 1)
    row_tok = jnp.where((j < counts[row_e]) & (row < pad_end[-1]), order[src] // TOP_K, T).astype(jnp.int32)
    x_pad = jnp.concatenate([x1.astype(bf16), jnp.zeros((1, D_MODEL), bf16)], axis=0)
    y_rows = _moe_rows(x_pad[row_tok], blk_e, n_used, layer, w1, b1, w2, b2, bm)
    pos = (pad_start[flat_e] + inv - start[flat_e]).astype(jnp.int32)
    pos_k = pos.reshape(T, TOP_K).T.reshape(-1)
    return y_rows[pos_k].reshape(TOP_K, T, D_MODEL), gates


def _ffn_combine(x_ref, y_refs, gt_ref, g_ref, b_ref):
    gt = gt_ref[...]
    y = y_refs[0][0] * gt[:, 0:1]
    for kk in range(1, TOP_K):
        y = y + y_refs[kk][0] * gt[:, kk:kk + 1]
    return _layer_norm_rows(DEEPNORM_ALPHA * x_ref[...] + y, g_ref[...], b_ref[...])


def _ffn_out_kernel(n_proj, x_ref, y0, y1, y2, y3, gt_ref, g_ref, b_ref, *rest):
    x2 = _ffn_combine(x_ref, (y0, y1, y2, y3), gt_ref, g_ref, b_ref)
    rest[n_proj][...] = x2
    xb = x2.astype(bf16)
    for p in range(n_proj):
        rest[n_proj + 1 + p][...] = _dot(xb, rest[p][...])


def _ffn_in_specs(tm, row_index):
    row = pl.BlockSpec((tm, D_MODEL), lambda *g: (row_index(*g), 0))
    yk = [pl.BlockSpec((1, tm, D_MODEL), functools.partial(lambda kk, *g: (kk, row_index(*g), 0), kk))
          for kk in range(TOP_K)]
    vec = pl.BlockSpec((1, D_MODEL), lambda *g: (0, 0))
    return row, [row] + yk + [pl.BlockSpec((tm, TOP_K), lambda *g: (row_index(*g), 0)), vec, vec]


def _ffn_out(x1, yk, gates, ln_g, ln_b, proj_ws):
    T = x1.shape[0]
    tm = min(256, T)
    row, specs = _ffn_in_specs(tm, lambda i: i)
    n_proj = len(proj_ws)
    return pl.pallas_call(
        functools.partial(_ffn_out_kernel, n_proj),
        grid=(T // tm,),
        in_specs=specs + [pl.BlockSpec(w.shape, lambda i: (0, 0)) for w in proj_ws],
        out_specs=[row] + [pl.BlockSpec((tm, w.shape[1]), lambda i: (i, 0)) for w in proj_ws],
        out_shape=[jax.ShapeDtypeStruct((T, D_MODEL), f32)] +
                  [jax.ShapeDtypeStruct((T, w.shape[1]), f32) for w in proj_ws],
        compiler_params=_cparams(("parallel",)),
        name="ffn_out_%d" % n_proj,
    )(x1, yk, yk, yk, yk, gates, ln_g.reshape(1, -1), ln_b.reshape(1, -1), *proj_ws)


def _ffn_out_seq_kernel(x_ref, y0, y1, y2, y3, gt_ref, g_ref, b_ref, wk_ref, wkt_ref, wv_ref, wvt_ref, wqt_ref,
                        x2_ref, kt_ref, v3_ref, kb_ref, vt_ref, qt_ref):
    x2 = _ffn_combine(x_ref, (y0, y1, y2, y3), gt_ref, g_ref, b_ref)
    x2_ref[...] = x2
    xb = x2.astype(bf16)
    kb_ref[...] = _dot(xb, wk_ref[...]).astype(bf16)
    kt_ref[0] = _dot_nt(wkt_ref[...], xb)
    vt_ref[0] = _dot_nt(wvt_ref[...], xb).astype(bf16)
    qt_ref[0] = _dot_nt(wqt_ref[...], xb).astype(bf16)
    v = _dot(xb, wv_ref[...])
    for h in range(H_B):
        v3_ref[:, h, :] = v[:, h * LANES:(h + 1) * LANES]


def _ffn_out_seq(x1, yk, gates, ln_g, ln_b, W, B, S):
    T = B * S
    tm = 256
    nj = S // tm
    row, specs = _ffn_in_specs(tm, lambda b, j: b * nj + j)
    wspec = pl.BlockSpec((D_MODEL, D_MODEL), lambda b, j: (0, 0), pipeline_mode=pl.Buffered(1))
    tspec = pl.BlockSpec((1, D_MODEL, tm), lambda b, j: (b, 0, j))
    return pl.pallas_call(
        _ffn_out_seq_kernel,
        grid=(B, nj),
        in_specs=specs + [wspec] * 5,
        out_specs=[row, tspec, pl.BlockSpec((tm, H_B, LANES), lambda b, j: (b * nj + j, 0, 0)), row, tspec, tspec],
        out_shape=[jax.ShapeDtypeStruct((T, D_MODEL), f32),
                   jax.ShapeDtypeStruct((B, D_MODEL, S), f32),
                   jax.ShapeDtypeStruct((T, H_B, LANES), f32),
                   jax.ShapeDtypeStruct((T, D_MODEL), bf16),
                   jax.ShapeDtypeStruct((B, D_MODEL, S), bf16),
                   jax.ShapeDtypeStruct((B, D_MODEL, S), bf16)],
        compiler_params=_cparams(("parallel", "parallel")),
        name="ffn_out_seq",
    )(x1, yk, yk, yk, yk, gates, ln_g.reshape(1, -1), ln_b.reshape(1, -1),
      W['w_k'], W['w_k'].T, W['w_v'], W['w_v'].T, W['w_q_scaled'].T)


ATT_T = 512


def _attn_seq_kernel(lam_init, slopes_ref, lam_ref, qt_ref, k_ref, vt_ref, g_ref, o_ref,
                     qt2_scr, bias_scr, bias_diag_scr, acc_scr):
    t = ATT_T
    h = pl.program_id(1)
    i = pl.program_id(2)
    slope = slopes_ref[h]

    @pl.when(i == 0)
    def _():
        kk = lax.broadcasted_iota(jnp.int32, (t, 2 * t), 0)
        qq = lax.broadcasted_iota(jnp.int32, (t, 2 * t), 1)
        qq = jnp.where(qq >= t, qq - t, qq)
        rel = slope * (kk - qq).astype(f32)
        bias_scr[...] = rel
        bias_diag_scr[...] = jnp.where(kk <= qq, rel, NEG_INF)

    qt = qt_ref[0]
    dim = lax.broadcasted_iota(jnp.int32, qt.shape, 0)
    zero = jnp.zeros_like(qt)
    qt2_scr[:, :t] = jnp.where(dim < HD_B, qt, zero)
    qt2_scr[:, t:] = jnp.where(dim >= HD_B, qt, zero)
    acc_scr[...] = jnp.zeros_like(acc_scr)

    def kv_block(j, bias_ref, m_old, l_old):
        c = pl.multiple_of(j * t, t)
        s = _dot(k_ref[pl.ds(c, t), :], qt2_scr[...]) + bias_ref[...]
        shift = slope * ((i - j) * t).astype(f32)
        m_new = jnp.maximum(m_old, jnp.max(s, 0, keepdims=True) - shift)
        p = jnp.exp(s - (m_new + shift))
        alpha = jnp.exp(m_old - m_new)
        l_new = alpha * l_old + jnp.sum(p, 0, keepdims=True)
        acc_scr[...] = alpha * acc_scr[...] + _dot(vt_ref[0, :, pl.ds(c, t)], p.astype(bf16))
        return m_new, l_new

    m0 = jnp.full((1, 2 * t), NEG_INF, f32)
    l0 = jnp.zeros((1, 2 * t), f32)
    m, l = lax.fori_loop(0, i, lambda j, ml: kv_block(j, bias_scr, *ml), (m0, l0))
    m, l = kv_block(i, bias_diag_scr, m, l)

    o = acc_scr[...] / l
    d = o[:, :t] - lam_ref[0] * o[:, t:]
    d = d * lax.rsqrt(jnp.mean(d * d, 0, keepdims=True) + LN_EPS) * g_ref[...]
    o_ref[...] = (d * (1.0 - lam_init)).T.astype(o_ref.dtype)


def _attn_seq(qt, kb, vt, lam, subln, lam_init, B, S):
    t = ATT_T
    nq = S // t
    slopes = jnp.exp2(-8.0 * jnp.arange(1, H_B + 1, dtype=f32) / H_B)
    smem = pl.BlockSpec(memory_space=pltpu.SMEM)
    return pl.pallas_call(
        functools.partial(_attn_seq_kernel, lam_init),
        grid=(B, H_B, nq),
        in_specs=[smem, smem,
                  pl.BlockSpec((1, LANES, t), lambda b, h, i: (b, h, i)),
                  pl.BlockSpec((S, LANES), lambda b, h, i: (b, h)),
                  pl.BlockSpec((1, LANES, S), lambda b, h, i: (b, h, 0)),
                  pl.BlockSpec((LANES, 1), lambda b, h, i: (0, 0))],
        out_specs=pl.BlockSpec((t, LANES), lambda b, h, i: (b * nq + i, h)),
        out_shape=jax.ShapeDtypeStruct((B * S, D_MODEL), bf16),
        scratch_shapes=[pltpu.VMEM((LANES, 2 * t), bf16),
                        pltpu.VMEM((t, 2 * t), f32),
                        pltpu.VMEM((t, 2 * t), f32),
                        pltpu.VMEM((LANES, 2 * t), f32)],
        compiler_params=_cparams(("parallel", "arbitrary", "arbitrary")),
        name="attn_seq",
    )(slopes, lam.reshape(1), qt, kb, vt, subln.reshape(-1, 1))


PAGES_PER_STEP = 8


def _attn_paged_kernel(lam_init, n_pages, pt_ref, slopes_ref, lam_ref, q_ref, kn_ref, vn_ref, g_ref, *rest):
    G = PAGES_PER_STEP
    k_refs = rest[:G]
    v_refs = rest[G:2 * G]
    o_ref, qbd_scr, m_scr, l_scr, acc_scr = rest[2 * G:]
    step = pl.program_id(1)
    nrow = 2 * H_B
    r1 = lax.broadcasted_iota(jnp.int32, (nrow, 1), 0)
    slope_col = jnp.zeros((nrow, 1), f32)
    for hh in range(H_B):
        slope_col = jnp.where(r1 // 2 == hh, slopes_ref[hh], slope_col)

    @pl.when(step == 0)
    def _():
        rowi = lax.broadcasted_iota(jnp.int32, (nrow, D_MODEL), 0)
        coli = lax.broadcasted_iota(jnp.int32, (nrow, D_MODEL), 1)
        own = (coli // HD_B) == rowi
        q = q_ref[0] * (HD_B ** -0.5)
        qbd = jnp.where(own, jnp.broadcast_to(q, (nrow, D_MODEL)), 0.0).astype(bf16)
        qbd_scr[...] = qbd
        m_scr[...] = jnp.sum(qbd.astype(f32) * kn_ref[0].astype(bf16).astype(f32), -1, keepdims=True)
        l_scr[...] = jnp.ones_like(l_scr)
        acc_scr[...] = jnp.broadcast_to(vn_ref[0].astype(bf16).astype(f32), (nrow, D_MODEL))

    tok = lax.broadcasted_iota(jnp.int32, (nrow, PAGE_SIZE), 1)
    m = m_scr[...]
    l = l_scr[...]
    acc = acc_scr[...]
    for u in range(G):
        page = step * G + u
        kb = k_refs[u][0].astype(bf16)
        vb = jnp.concatenate([v_refs[u][0, :, hh, :] for hh in range(H_B)], axis=1).astype(bf16)
        dist = ((n_pages - page) * PAGE_SIZE - tok).astype(f32)
        s = _dot(qbd_scr[...], kb) - slope_col * dist
        m_new = jnp.maximum(m, jnp.max(s, -1, keepdims=True))
        p = jnp.exp(s - m_new)
        alpha = jnp.exp(m - m_new)
        l = alpha * l + jnp.sum(p, -1, keepdims=True)
        acc = alpha * acc + _dot(p.astype(bf16), vb)
        m = m_new
    m_scr[...] = m
    l_scr[...] = l
    acc_scr[...] = acc

    @pl.when(step == pl.num_programs(1) - 1)
    def _():
        o = acc / l
        for hh in range(H_B):
            blk = o[2 * hh:2 * hh + 2, hh * LANES:(hh + 1) * LANES]
            d = blk[0:1, :] - lam_ref[0] * blk[1:2, :]
            d = d * lax.rsqrt(jnp.mean(d * d, -1, keepdims=True) + LN_EPS) * g_ref[...]
            o_ref[0, :, hh * LANES:(hh + 1) * LANES] = (d * (1.0 - lam_init)).astype(o_ref.dtype)


def _attn_paged(q, k_new, v_new, cache_kt, cache_v, page_table, lam, subln, lam_init):
    B = q.shape[0]
    G = PAGES_PER_STEP
    n_pages = page_table.shape[1]
    slopes = jnp.exp2(-8.0 * jnp.arange(1, H_B + 1, dtype=f32) / H_B)
    smem = pl.BlockSpec(memory_space=pltpu.SMEM)
    vec = pl.BlockSpec((1, 1, D_MODEL), lambda b, s, pt: (b, 0, 0))

    def page_index(u, ndim):
        return lambda b, s, pt: (pt[b * n_pages + s * G + u],) + (0,) * (ndim - 1)

    k_specs = [pl.BlockSpec((1, D_MODEL, PAGE_SIZE), page_index(u, 3)) for u in range(G)]
    v_specs = [pl.BlockSpec((1, PAGE_SIZE, H_B, LANES), page_index(u, 4)) for u in range(G)]
    grid_spec = pltpu.PrefetchScalarGridSpec(
        num_scalar_prefetch=1,
        grid=(B, n_pages // G),
        in_specs=[smem, smem, vec, vec, vec, pl.BlockSpec((1, LANES), lambda b, s, pt: (0, 0))] + k_specs + v_specs,
        out_specs=vec,
        scratch_shapes=[pltpu.VMEM((2 * H_B, D_MODEL), bf16),
                        pltpu.VMEM((2 * H_B, 1), f32),
                        pltpu.VMEM((2 * H_B, 1), f32),
                        pltpu.VMEM((2 * H_B, D_MODEL), f32)],
    )
    d = pl.pallas_call(
        functools.partial(_attn_paged_kernel, lam_init, n_pages),
        grid_spec=grid_spec,
        out_shape=jax.ShapeDtypeStruct((B, 1, D_MODEL), bf16),
        compiler_params=_cparams(("parallel", "arbitrary")),
        name="attn_paged",
    )(page_table.reshape(-1), slopes, lam.reshape(1), q.reshape(B, 1, -1), k_new.reshape(B, 1, -1),
      v_new.reshape(B, 1, -1), subln.reshape(1, -1), *([cache_kt] * G), *([cache_v] * G))
    return d.reshape(B, D_MODEL)


def _layer0_front(x2d, hgrn, W, moe_rows):
    qs, k, lf, v, gs = _inproj(x2d, W['w_in'], W['lb'])
    o, state = hgrn(qs, k, lf, v)
    x1, logits = _mix_out(o, (gs, W['gnorm']), x2d, W['w_o_a'], W['ln_mix_g'][0], W['ln_mix_b'][0],
                          W['w_router'][0], W['b_router'][0])
    yk, gates = _moe_ffn(x1, logits, 0, W['w1'], W['b1'], W['w2'], W['b2'], moe_rows)
    return x1, yk, gates, state


def _layer1_back(d, x2, W, moe_rows):
    x3, logits = _mix_out(d, None, x2, W['w_o_b'], W['ln_mix_g'][1], W['ln_mix_b'][1],
                          W['w_router'][1], W['b_router'][1])
    yk, gates = _moe_ffn(x3, logits, 1, W['w1'], W['b1'], W['w2'], W['b2'], moe_rows)
    (x4,) = _ffn_out(x3, yk, gates, W['ln_ffn_g'][1], W['ln_ffn_b'][1], [])
    return x4


def _trunk_prompt(x, W, lam, subln, lam_init):
    B, S, _ = x.shape
    x2d = x.reshape(B * S, D_MODEL)
    x1, yk, gates, state = _layer0_front(x2d, lambda qs, k, lf, v: _hgrn_seq(qs, k, lf, v, B, S), W, MOE_ROWS)
    x2, kt, v3, kb, vt, qt = _ffn_out_seq(x1, yk, gates, W['ln_ffn_g'][0], W['ln_ffn_b'][0], W, B, S)
    d = _attn_seq(qt, kb, vt, lam, subln, lam_init, B, S)
    x4 = _layer1_back(d, x2, W, MOE_ROWS)
    k_out = jnp.transpose(kt.reshape(B, H_B, 2, HD_B, S), (0, 4, 1, 2, 3))
    return x4.reshape(B, S, D_MODEL), state[None], k_out, v3.reshape(B, S, H_B, 2 * HD_B)


def _trunk_sample(x, state0, cache_kt, cache_v, page_table, W, lam, subln, lam_init):
    B = x.shape[0]
    x2d = x.reshape(B, D_MODEL)
    x1, yk, gates, state = _layer0_front(x2d, lambda qs, k, lf, v: _hgrn_step(state0, qs, k, lf, v), W, 32)
    x2, k_sh, v_sh, q = _ffn_out(x1, yk, gates, W['ln_ffn_g'][0], W['ln_ffn_b'][0],
                                 [W['w_k'], W['w_v'], W['w_q']])
    d = _attn_paged(q, k_sh, v_sh, cache_kt, cache_v, page_table, lam, subln, lam_init)
    x4 = _layer1_back(d, x2, W, 32)
    return (x4.reshape(B, 1, D_MODEL), state[None],
            k_sh.reshape(B, 1, H_B, 2, HD_B), v_sh.reshape(B, 1, H_B, 2 * HD_B))


def kernel(x_prompt, x_sample, state_hgrn, cache_k, cache_v, page_table, w_in_a, lb_logits, gnorm_a, w_o_a,
           w_kv, w_q_b, lam_q1, lam_k1, lam_q2, lam_k2, subln_b, w_o_b, ln_mix_g, ln_mix_b, ln_ffn_g, ln_ffn_b,
           w_router, b_router, w1, b1, w2, b2):
    q_dim = H_B * 2 * HD_B
    lb = jnp.cumsum(jax.nn.softmax(lb_logits.astype(f32), axis=0), axis=0)[:N_A_LAYERS]
    W = dict(
        lb=lb[0].reshape(1, D_A), w_in=w_in_a[0].astype(bf16), gnorm=gnorm_a[0], w_o_a=w_o_a[0].astype(bf16),
        w_k=w_kv[:, :q_dim].astype(bf16), w_v=w_kv[:, q_dim:].astype(bf16), w_q=w_q_b[0].astype(bf16),
        w_q_scaled=(w_q_b[0] * (HD_B ** -0.5)).astype(bf16),
        w_o_b=w_o_b[0].astype(bf16), ln_mix_g=ln_mix_g, ln_mix_b=ln_mix_b, ln_ffn_g=ln_ffn_g, ln_ffn_b=ln_ffn_b,
        w_router=w_router, b_router=b_router, w1=w1, b1=b1, w2=w2, b2=b2)
    layer = N_A_LAYERS
    lam_init = 0.8 - 0.6 * math.exp(-0.3 * layer)
    lam = (jnp.exp(jnp.sum((lam_q1[0] * lam_k1[0]).astype(f32)))
           - jnp.exp(jnp.sum((lam_q2[0] * lam_k2[0]).astype(f32))) + lam_init)
    subln = subln_b[0]

    y_p, st_p, k_p, v_p = _trunk_prompt(x_prompt, W, lam, subln, lam_init)

    n_pool = cache_k.shape[0]
    cache_kt = jnp.transpose(cache_k, (0, 2, 3, 4, 1)).reshape(n_pool, D_MODEL, PAGE_SIZE)
    y_s, st_s, k_s, v_s = _trunk_sample(x_sample, state_hgrn[0], cache_kt, cache_v, page_table, W,
                                        lam, subln, lam_init)
    return (y_p, y_s, st_p, st_s, k_p, v_p, k_s, v_s)
```

```python
import functools
import math

import numpy as np
import jax
import jax.numpy as jnp
from jax import lax
from jax.experimental import pallas as pl
from jax.experimental.pallas import tpu as pltpu

f32 = jnp.float32
bf16 = jnp.bfloat16

D_MODEL = 1024
DEPTH = 2
N_A_LAYERS = 1
H_A = 8
DK_A = 128
DV_A = 128
D_A = H_A * DK_A
HD_B = 64
H_B = 8
N_EXPERTS = 32
TOP_K = 4
D_FF = D_MODEL
SWIGLU_LIMIT = 7.0
SWIGLU_ALPHA = 1.702
DEEPNORM_ALPHA = (2.0 * DEPTH) ** 0.25
LN_EPS = 1e-5
NEG_INF = -1e30
PAGE_SIZE = 128
LOG2_E = 1.4426950408889634

LANES = 128
VMEM_LIMIT = 56 * 1024 * 1024
MOE_VMEM_LIMIT = 60 * 1024 * 1024

HGRN_TILE = 128
HGRN_HEADS_PER_STEP = 2
MOE_ROWS = 512


def _cparams(sem):
    return pltpu.CompilerParams(dimension_semantics=sem, vmem_limit_bytes=VMEM_LIMIT)


def _sigmoid(x):
    return 1.0 / (1.0 + jnp.exp(-x))


def _layer_norm_rows(x, g, b):
    mu = jnp.mean(x, -1, keepdims=True)
    xc = x - mu
    var = jnp.mean(xc * xc, -1, keepdims=True)
    return xc * lax.rsqrt(var + LN_EPS) * g + b


def _dot(a, b):
    return jnp.dot(a, b, preferred_element_type=f32)


def _dot_nt(a, b):
    return lax.dot_general(a, b, (((1,), (1,)), ((), ())), preferred_element_type=f32)


def _dot_tn(a, b):
    return lax.dot_general(a, b, (((0,), (0,)), ((), ())), preferred_element_type=f32)


def _inproj_kernel(x_ref, w_ref, lb_ref, qs_ref, k_ref, lf_ref, v_ref, gs_ref):
    h = _dot(x_ref[...].astype(bf16), w_ref[...])
    q = h[:, :D_A]
    fr = h[:, D_A:2 * D_A]
    lb = lb_ref[...]
    z = jnp.exp(-jnp.abs(fr))
    r = 1.0 / (1.0 + z)
    zr = z * r
    pos = fr >= 0
    sig = jnp.where(pos, r, zr)
    nsig = jnp.where(pos, zr, r)
    qs_ref[...] = q * _sigmoid(q)
    k_ref[...] = (1.0 - lb) * nsig
    lf_ref[...] = jnp.log(lb + (1.0 - lb) * sig)
    v_ref[...] = h[:, 2 * D_A:3 * D_A]
    g = h[:, 3 * D_A:]
    gs_ref[...] = g * _sigmoid(g)


def _inproj(x2d, w_in, lb, row0, T):
    tm = min(256, T)
    blk0 = row0 // tm
    out = jax.ShapeDtypeStruct((T, D_A), f32)
    row = pl.BlockSpec((tm, D_A), lambda i: (i, 0))
    return pl.pallas_call(
        _inproj_kernel,
        grid=(T // tm,),
        in_specs=[pl.BlockSpec((tm, D_MODEL), lambda i: (i + blk0, 0)),
                  pl.BlockSpec((D_MODEL, 4 * D_A), lambda i: (0, 0)),
                  pl.BlockSpec((1, D_A), lambda i: (0, 0))],
        out_specs=[row] * 5,
        out_shape=[out] * 5,
        compiler_params=_cparams(("parallel",)),
        name="hgrn_inproj",
    )(x2d, w_in, lb)


def _hgrn_levels(R):
    n = R // 2
    out = []
    while n >= 1:
        out.append(n)
        n //= 2
    return out


def _hgrn_masks(R):
    t = np.arange(R)[:, None]
    s = np.arange(R)[None, :]
    ms = []
    for n in _hgrn_levels(R):
        ms.append((t // (2 * n) == s // (2 * n)) & (t % (2 * n) >= n) & (s % (2 * n) < n))
    ms.append(t == s)
    return np.stack(ms).astype(np.float32)


def _level_ref_rows(b, n):
    R = b.shape[0]
    if 2 * n >= 8:
        b3 = b.reshape(R // (2 * n), 2 * n, LANES)
        return jnp.broadcast_to(b3[:, n - 1:n, :], b3.shape).reshape(R, LANES)
    b3 = b.reshape(R // 8, 8, LANES)
    sub = lax.broadcasted_iota(jnp.int32, b3.shape, 1)
    out = None
    for r0 in range(0, 8, 2 * n):
        row = jnp.broadcast_to(b3[:, r0 + n - 1:r0 + n, :], b3.shape)
        out = row if out is None else jnp.where(sub >= r0, row, out)
    return out.reshape(R, LANES)


def _split3(x):
    hi = x.astype(bf16)
    r1 = x - hi.astype(f32)
    mid = r1.astype(bf16)
    lo = (r1 - mid.astype(f32)).astype(bf16)
    return hi, mid, lo


def _hgrn_seq_kernel(qs_ref, k_ref, lf_ref, v_ref, masks_ref, tril_ref, o_ref, st_ref, state_t):
    R = HGRN_TILE
    levels = _hgrn_levels(R)
    step = pl.program_id(2)

    @pl.when(step == 0)
    def _():
        state_t[...] = jnp.zeros_like(state_t)

    def tile_head(r, hh):
        cols = slice(hh * LANES, (hh + 1) * LANES)
        q = qs_ref[pl.ds(r, R), cols]
        k = k_ref[pl.ds(r, R), cols]
        v = v_ref[pl.ds(r, R), cols].astype(bf16)
        hi, mid, lo = _split3(lf_ref[pl.ds(r, R), cols] * LOG2_E)
        c3 = _dot(tril_ref[...], jnp.concatenate([hi, mid, lo], axis=1))
        b = c3[:, :LANES] + c3[:, LANES:2 * LANES] + c3[:, 2 * LANES:]
        a = masks_ref[len(levels)] * _dot_nt(q.astype(bf16), k.astype(bf16))
        for li, n in enumerate(levels):
            e = jnp.exp2(-jnp.abs(b - _level_ref_rows(b, n)))
            a = a + masks_ref[li] * _dot_nt((q * e).astype(bf16), (k * e).astype(bf16))
        st = state_t[hh]
        b_last = b[R - 1:R, :]
        qd = (q * jnp.exp2(b)).astype(bf16)
        kd = (k * jnp.exp2(b_last - b)).astype(bf16)
        o_ref[pl.ds(r, R), cols] = _dot(a.astype(bf16), v) + _dot_nt(qd, st.astype(bf16))
        state_t[hh] = st * jnp.exp2(b_last) + _dot_tn(v, kd)

    def tile(i, carry):
        r = pl.multiple_of(i * R, R)
        for hh in range(HGRN_HEADS_PER_STEP):
            tile_head(r, hh)
        return carry

    lax.fori_loop(0, qs_ref.shape[0] // R, tile, 0)

    @pl.when(step == pl.num_programs(2) - 1)
    def _():
        st_ref[0] = state_t[...]


def _hgrn_seq(qs, k, lf, v, B, S):
    R = HGRN_TILE
    G = HGRN_HEADS_PER_STEP
    rb = min(512, S)
    nst = S // rb
    masks = jnp.asarray(_hgrn_masks(R))
    tril = jnp.asarray(np.tril(np.ones((R, R), np.float32)), dtype=bf16)
    row = pl.BlockSpec((rb, G * LANES), lambda b, h, j: (b * nst + j, h))
    o, st_t = pl.pallas_call(
        _hgrn_seq_kernel,
        grid=(B, H_A // G, nst),
        in_specs=[row, row, row, row,
                  pl.BlockSpec(masks.shape, lambda b, h, j: (0, 0, 0)),
                  pl.BlockSpec((R, R), lambda b, h, j: (0, 0))],
        out_specs=[row, pl.BlockSpec((1, G, DV_A, DK_A), lambda b, h, j: (b, h, 0, 0))],
        out_shape=[jax.ShapeDtypeStruct((B * S, D_A), f32),
                   jax.ShapeDtypeStruct((B, H_A, DV_A, DK_A), f32)],
        scratch_shapes=[pltpu.VMEM((G, DV_A, DK_A), f32)],
        compiler_params=_cparams(("parallel", "parallel", "arbitrary")),
        name="hgrn_seq",
    )(qs, k, lf, v, masks, tril)
    return o, jnp.swapaxes(st_t, 2, 3)


def _hgrn_step_kernel(s_ref, q_ref, k_ref, lf_ref, v_ref, o_ref, so_ref):
    for h in range(H_A):
        s_new = jnp.exp(lf_ref[0, h]) * s_ref[0, h] + k_ref[0, h] * v_ref[0, h]
        so_ref[0, h] = s_new
        o_ref[0, h] = jnp.sum(q_ref[0, h] * s_new, axis=0, keepdims=True)


def _hgrn_step(state, qs, k, lf, v):
    B = state.shape[0]
    col = lambda t: t.reshape(B, H_A, DK_A, 1)
    cspec = pl.BlockSpec((1, H_A, DK_A, 1), lambda b: (b, 0, 0, 0))
    rspec = pl.BlockSpec((1, H_A, 1, DV_A), lambda b: (b, 0, 0, 0))
    sspec = pl.BlockSpec((1, H_A, DK_A, DV_A), lambda b: (b, 0, 0, 0))
    o, s_new = pl.pallas_call(
        _hgrn_step_kernel,
        grid=(B,),
        in_specs=[sspec, cspec, cspec, cspec, rspec],
        out_specs=[rspec, sspec],
        out_shape=[jax.ShapeDtypeStruct((B, H_A, 1, DV_A), f32),
                   jax.ShapeDtypeStruct(state.shape, f32)],
        compiler_params=_cparams(("parallel",)),
        name="hgrn_step",
    )(state, col(qs), col(k), col(lf), v.reshape(B, H_A, 1, DV_A))
    return o.reshape(B, D_A), s_new


def _mix_out_kernel(gated, *refs):
    if gated:
        m_ref, gs_ref, gn_ref, x_ref, wo_ref, lg_ref, lbias_ref, wr_ref, br_ref, x1_ref, logit_ref = refs
        m = m_ref[...] * gs_ref[...]
        m = m * lax.rsqrt(jnp.mean(m * m, -1, keepdims=True) + LN_EPS) * gn_ref[...]
        m = m.astype(bf16)
    else:
        m_ref, x_ref, wo_ref, lg_ref, lbias_ref, wr_ref, br_ref, x1_ref, logit_ref = refs
        m = m_ref[...]
    mix = _dot(m, wo_ref[...])
    x1 = _layer_norm_rows(DEEPNORM_ALPHA * x_ref[...] + mix, lg_ref[...], lbias_ref[...])
    x1_ref[...] = x1
    logit_ref[...] = jnp.dot(x1, wr_ref[...], preferred_element_type=f32,
                             precision=lax.Precision.HIGHEST) + br_ref[...]


def _mix_out(m, gate, x2d, w_o, ln_g, ln_b, w_router, b_router, row0=0):
    T = m.shape[0]
    tm = min(256, T)
    blk0 = row0 // tm
    row = pl.BlockSpec((tm, D_MODEL), lambda i: (i, 0))
    vec = pl.BlockSpec((1, D_MODEL), lambda i: (0, 0))
    gated = gate is not None
    ins = [m] + ([gate[0], gate[1].reshape(1, -1)] if gated else []) + [
        x2d, w_o, ln_g.reshape(1, -1), ln_b.reshape(1, -1), w_router, b_router.reshape(1, -1)]
    specs = [row] + ([row, vec] if gated else []) + [
        pl.BlockSpec((tm, D_MODEL), lambda i: (i + blk0, 0)),
        pl.BlockSpec((D_MODEL, D_MODEL), lambda i: (0, 0)), vec, vec,
        pl.BlockSpec((D_MODEL, N_EXPERTS), lambda i: (0, 0)),
        pl.BlockSpec((1, N_EXPERTS), lambda i: (0, 0))]
    return pl.pallas_call(
        functools.partial(_mix_out_kernel, gated),
        grid=(T // tm,),
        in_specs=specs,
        out_specs=[row, pl.BlockSpec((tm, N_EXPERTS), lambda i: (i, 0))],
        out_shape=[jax.ShapeDtypeStruct((T, D_MODEL), f32),
                   jax.ShapeDtypeStruct((T, N_EXPERTS), f32)],
        compiler_params=_cparams(("parallel",)),
        name="mix_out_gated" if gated else "mix_out",
    )(*ins)


MOE_FF_CHUNK = 512


def _moe_kernel(blk_e_ref, n_used_ref, x_ref, w1_ref, b1_ref, w2_ref, b2_ref, y_ref, w1b, w2b):
    i = pl.program_id(0)
    live = i < n_used_ref[0]
    prev = blk_e_ref[jnp.maximum(i - 1, 0)]

    @pl.when(live & ((i == 0) | (blk_e_ref[i] != prev)))
    def _():
        w1b[...] = w1_ref[0, 0].astype(bf16)
        w2b[...] = w2_ref[0, 0].astype(bf16)

    @pl.when(live)
    def _():
        x = x_ref[...]
        y = None
        for c in range(0, D_FF, MOE_FF_CHUNK):
            hg = _dot(x, w1b[:, c:c + MOE_FF_CHUNK]) + b1_ref[0, 0, :, c:c + MOE_FF_CHUNK]
            hl = (_dot(x, w1b[:, D_FF + c:D_FF + c + MOE_FF_CHUNK])
                  + b1_ref[0, 0, :, D_FF + c:D_FF + c + MOE_FF_CHUNK])
            x_glu = jnp.minimum(hg, SWIGLU_LIMIT)
            x_lin = jnp.clip(hl, -SWIGLU_LIMIT, SWIGLU_LIMIT)
            act = x_glu * _sigmoid(SWIGLU_ALPHA * x_glu) * (x_lin + 1.0)
            part = _dot(act.astype(bf16), w2b[c:c + MOE_FF_CHUNK, :])
            y = part if y is None else y + part
        y_ref[...] = (y + b2_ref[0, 0]).astype(y_ref.dtype)


def _moe_rows(x_rows, blk_e, n_used, layer, w1, b1, w2, b2, bm):
    nb = x_rows.shape[0] // bm
    grid_spec = pltpu.PrefetchScalarGridSpec(
        num_scalar_prefetch=2,
        grid=(nb,),
        in_specs=[pl.BlockSpec((bm, D_MODEL), lambda i, e, n: (i, 0)),
                  pl.BlockSpec((1, 1, D_MODEL, 2 * D_FF), lambda i, e, n: (layer, e[i], 0, 0)),
                  pl.BlockSpec((1, 1, 1, 2 * D_FF), lambda i, e, n: (layer, e[i], 0, 0)),
                  pl.BlockSpec((1, 1, D_FF, D_MODEL), lambda i, e, n: (layer, e[i], 0, 0)),
                  pl.BlockSpec((1, 1, 1, D_MODEL), lambda i, e, n: (layer, e[i], 0, 0))],
        out_specs=pl.BlockSpec((bm, D_MODEL), lambda i, e, n: (i, 0)),
        scratch_shapes=[pltpu.VMEM((D_MODEL, 2 * D_FF), bf16), pltpu.VMEM((D_FF, D_MODEL), bf16)],
    )
    return pl.pallas_call(
        _moe_kernel,
        grid_spec=grid_spec,
        out_shape=jax.ShapeDtypeStruct((nb * bm, D_MODEL), bf16),
        compiler_params=pltpu.CompilerParams(dimension_semantics=("arbitrary",),
                                             vmem_limit_bytes=MOE_VMEM_LIMIT),
        name="moe_experts",
    )(blk_e, n_used, x_rows, w1, b1.reshape(DEPTH, N_EXPERTS, 1, -1), w2, b2.reshape(DEPTH, N_EXPERTS, 1, -1))


def _moe_ffn(x1, logits, layer, w1, b1, w2, b2, bm):
    T = x1.shape[0]
    n_slots = T * TOP_K
    top_val, top_idx = lax.top_k(logits, TOP_K)
    gates = jax.nn.softmax(top_val, axis=-1)
    flat_e = top_idx.reshape(-1)
    order = jnp.argsort(flat_e)
    inv = jnp.argsort(order)
    counts = jnp.bincount(flat_e, length=N_EXPERTS)
    padded = (counts + bm - 1) // bm * bm
    pad_end = jnp.cumsum(padded)
    pad_start = pad_end - padded
    start = jnp.cumsum(counts) - counts
    n_blocks = (n_slots + N_EXPERTS * (bm - 1) + bm - 1) // bm
    blk_e = jnp.minimum(jnp.searchsorted(pad_end, jnp.arange(n_blocks) * bm, side='right'),
                        N_EXPERTS - 1).astype(jnp.int32)
    n_used = (pad_end[-1] // bm).astype(jnp.int32).reshape(1)
    row = jnp.arange(n_blocks * bm)
    row_e = jnp.repeat(blk_e, bm)
    j = row - pad_start[row_e]
    src = jnp.clip(start[row_e] + j, 0, n_slots - 1)
    row_tok = jnp.where((j < counts[row_e]) & (row < pad_end[-1]), order[src] // TOP_K, T).astype(jnp.int32)
    x_pad = jnp.concatenate([x1.astype(bf16), jnp.zeros((1, D_MODEL), bf16)], axis=0)
    y_rows = _moe_rows(x_pad[row_tok], blk_e, n_used, layer, w1, b1, w2, b2, bm)
    pos = (pad_start[flat_e] + inv - start[flat_e]).astype(jnp.int32)
    pos_k = pos.reshape(T, TOP_K).T.reshape(-1)
    return y_rows[pos_k].reshape(TOP_K, T, D_MODEL), gates


def _ffn_combine(x_ref, y_refs, gt_ref, g_ref, b_ref):
    gt = gt_ref[...]
    y = y_refs[0][0].astype(f32) * gt[:, 0:1]
    for kk in range(1, TOP_K):
        y = y + y_refs[kk][0].astype(f32) * gt[:, kk:kk + 1]
    return _layer_norm_rows(DEEPNORM_ALPHA * x_ref[...] + y, g_ref[...], b_ref[...])


def _ffn_out_kernel(n_proj, x_ref, y0, y1, y2, y3, gt_ref, g_ref, b_ref, *rest):
    x2 = _ffn_combine(x_ref, (y0, y1, y2, y3), gt_ref, g_ref, b_ref)
    rest[n_proj][...] = x2
    xb = x2.astype(bf16)
    for p in range(n_proj):
        rest[n_proj + 1 + p][...] = _dot(xb, rest[p][...])


def _ffn_in_specs(tm, row_index):
    row = pl.BlockSpec((tm, D_MODEL), lambda *g: (row_index(*g), 0))
    yk = [pl.BlockSpec((1, tm, D_MODEL), functools.partial(lambda kk, *g: (kk, row_index(*g), 0), kk))
          for kk in range(TOP_K)]
    vec = pl.BlockSpec((1, D_MODEL), lambda *g: (0, 0))
    return row, [row] + yk + [pl.BlockSpec((tm, TOP_K), lambda *g: (row_index(*g), 0)), vec, vec]


def _ffn_out(x1, yk, gates, ln_g, ln_b, proj_ws):
    T = x1.shape[0]
    tm = min(256, T)
    row, specs = _ffn_in_specs(tm, lambda i: i)
    n_proj = len(proj_ws)
    return pl.pallas_call(
        functools.partial(_ffn_out_kernel, n_proj),
        grid=(T // tm,),
        in_specs=specs + [pl.BlockSpec(w.shape, lambda i: (0, 0)) for w in proj_ws],
        out_specs=[row] + [pl.BlockSpec((tm, w.shape[1]), lambda i: (i, 0)) for w in proj_ws],
        out_shape=[jax.ShapeDtypeStruct((T, D_MODEL), f32)] +
                  [jax.ShapeDtypeStruct((T, w.shape[1]), f32) for w in proj_ws],
        compiler_params=_cparams(("parallel",)),
        name="ffn_out_%d" % n_proj,
    )(x1, yk, yk, yk, yk, gates, ln_g.reshape(1, -1), ln_b.reshape(1, -1), *proj_ws)


def _ffn_out_seq_kernel(x_ref, y0, y1, y2, y3, gt_ref, g_ref, b_ref, wk_ref, wkt_ref, wv_ref, wvt_ref, wqt_ref,
                        x2_ref, kt_ref, v3_ref, kb_ref, vt_ref, qt_ref):
    x2 = _ffn_combine(x_ref, (y0, y1, y2, y3), gt_ref, g_ref, b_ref)
    x2_ref[...] = x2
    xb = x2.astype(bf16)
    kb_ref[...] = _dot(xb, wk_ref[...]).astype(bf16)
    kt_ref[0] = _dot_nt(wkt_ref[...], xb)
    vt_ref[0] = _dot_nt(wvt_ref[...], xb).astype(bf16)
    qt_ref[0] = _dot_nt(wqt_ref[...], xb).astype(bf16)
    v = _dot(xb, wv_ref[...])
    for h in range(H_B):
        v3_ref[:, h, :] = v[:, h * LANES:(h + 1) * LANES]


def _ffn_out_seq(x1, yk, gates, ln_g, ln_b, W, B, S):
    T = B * S
    tm = 256
    nj = S // tm
    row, specs = _ffn_in_specs(tm, lambda b, j: b * nj + j)
    wspec = pl.BlockSpec((D_MODEL, D_MODEL), lambda b, j: (0, 0), pipeline_mode=pl.Buffered(1))
    tspec = pl.BlockSpec((1, D_MODEL, tm), lambda b, j: (b, 0, j))
    return pl.pallas_call(
        _ffn_out_seq_kernel,
        grid=(B, nj),
        in_specs=specs + [wspec] * 5,
        out_specs=[row, tspec, pl.BlockSpec((tm, H_B, LANES), lambda b, j: (b * nj + j, 0, 0)), row, tspec, tspec],
        out_shape=[jax.ShapeDtypeStruct((T, D_MODEL), f32),
                   jax.ShapeDtypeStruct((B, D_MODEL, S), f32),
                   jax.ShapeDtypeStruct((T, H_B, LANES), f32),
                   jax.ShapeDtypeStruct((T, D_MODEL), bf16),
                   jax.ShapeDtypeStruct((B, D_MODEL, S), bf16),
                   jax.ShapeDtypeStruct((B, D_MODEL, S), bf16)],
        compiler_params=_cparams(("parallel", "parallel")),
        name="ffn_out_seq",
    )(x1, yk, yk, yk, yk, gates, ln_g.reshape(1, -1), ln_b.reshape(1, -1),
      W['w_k'], W['w_k'].T, W['w_v'], W['w_v'].T, W['w_q_scaled'].T)


ATT_T = 512


def _attn_seq_kernel(lam_init, slopes_ref, lam_ref, qt_ref, k_ref, vt_ref, g_ref, o_ref,
                     qt2_scr, bias_scr, bias_diag_scr, acc_scr):
    t = ATT_T
    h = pl.program_id(1)
    i = pl.program_id(2)
    slope = slopes_ref[h]

    @pl.when(i == 0)
    def _():
        kk = lax.broadcasted_iota(jnp.int32, (t, 2 * t), 0)
        qq = lax.broadcasted_iota(jnp.int32, (t, 2 * t), 1)
        qq = jnp.where(qq >= t, qq - t, qq)
        rel = slope * (kk - qq).astype(f32)
        bias_scr[...] = rel
        bias_diag_scr[...] = jnp.where(kk <= qq, rel, NEG_INF)

    qt = qt_ref[0]
    dim = lax.broadcasted_iota(jnp.int32, qt.shape, 0)
    zero = jnp.zeros_like(qt)
    qt2_scr[:, :t] = jnp.where(dim < HD_B, qt, zero)
    qt2_scr[:, t:] = jnp.where(dim >= HD_B, qt, zero)
    acc_scr[...] = jnp.zeros_like(acc_scr)

    def kv_block(j, bias_ref, m_old, l_old):
        c = pl.multiple_of(j * t, t)
        s = _dot(k_ref[pl.ds(c, t), :], qt2_scr[...]) + bias_ref[...]
        shift = slope * ((i - j) * t).astype(f32)
        m_new = jnp.maximum(m_old, jnp.max(s, 0, keepdims=True) - shift)
        p = jnp.exp(s - (m_new + shift))
        alpha = jnp.exp(m_old - m_new)
        l_new = alpha * l_old + jnp.sum(p, 0, keepdims=True)
        acc_scr[...] = alpha * acc_scr[...] + _dot(vt_ref[0, :, pl.ds(c, t)], p.astype(bf16))
        return m_new, l_new

    m0 = jnp.full((1, 2 * t), NEG_INF, f32)
    l0 = jnp.zeros((1, 2 * t), f32)
    m, l = lax.fori_loop(0, i, lambda j, ml: kv_block(j, bias_scr, *ml), (m0, l0))
    m, l = kv_block(i, bias_diag_scr, m, l)

    o = acc_scr[...] / l
    d = o[:, :t] - lam_ref[0] * o[:, t:]
    d = d * lax.rsqrt(jnp.mean(d * d, 0, keepdims=True) + LN_EPS) * g_ref[...]
    o_ref[...] = (d * (1.0 - lam_init)).T.astype(o_ref.dtype)


def _attn_seq(qt, kb, vt, lam, subln, lam_init, B, S):
    t = ATT_T
    nq = S // t
    slopes = jnp.exp2(-8.0 * jnp.arange(1, H_B + 1, dtype=f32) / H_B)
    smem = pl.BlockSpec(memory_space=pltpu.SMEM)
    return pl.pallas_call(
        functools.partial(_attn_seq_kernel, lam_init),
        grid=(B, H_B, nq),
        in_specs=[smem, smem,
                  pl.BlockSpec((1, LANES, t), lambda b, h, i: (b, h, i)),
                  pl.BlockSpec((S, LANES), lambda b, h, i: (b, h)),
                  pl.BlockSpec((1, LANES, S), lambda b, h, i: (b, h, 0)),
                  pl.BlockSpec((LANES, 1), lambda b, h, i: (0, 0))],
        out_specs=pl.BlockSpec((t, LANES), lambda b, h, i: (b * nq + i, h)),
        out_shape=jax.ShapeDtypeStruct((B * S, D_MODEL), bf16),
        scratch_shapes=[pltpu.VMEM((LANES, 2 * t), bf16),
                        pltpu.VMEM((t, 2 * t), f32),
                        pltpu.VMEM((t, 2 * t), f32),
                        pltpu.VMEM((LANES, 2 * t), f32)],
        compiler_params=_cparams(("parallel", "arbitrary", "arbitrary")),
        name="attn_seq",
    )(slopes, lam.reshape(1), qt, kb, vt, subln.reshape(-1, 1))


PAGES_PER_STEP = 8


def _attn_paged_kernel(lam_init, n_pages, pt_ref, slopes_ref, lam_ref, q_ref, kn_ref, vn_ref, g_ref, *rest):
    G = PAGES_PER_STEP
    k_refs = rest[:G]
    v_refs = rest[G:2 * G]
    o_ref, qbd_scr, m_scr, l_scr, acc_scr = rest[2 * G:]
    step = pl.program_id(1)
    nrow = 2 * H_B
    r1 = lax.broadcasted_iota(jnp.int32, (nrow, 1), 0)
    slope_col = jnp.zeros((nrow, 1), f32)
    for hh in range(H_B):
        slope_col = jnp.where(r1 // 2 == hh, slopes_ref[hh], slope_col)

    @pl.when(step == 0)
    def _():
        rowi = lax.broadcasted_iota(jnp.int32, (nrow, D_MODEL), 0)
        coli = lax.broadcasted_iota(jnp.int32, (nrow, D_MODEL), 1)
        own = (coli // HD_B) == rowi
        q = q_ref[0] * (HD_B ** -0.5)
        qbd = jnp.where(own, jnp.broadcast_to(q, (nrow, D_MODEL)), 0.0).astype(bf16)
        qbd_scr[...] = qbd
        m_scr[...] = jnp.sum(qbd.astype(f32) * kn_ref[0].astype(bf16).astype(f32), -1, keepdims=True)
        l_scr[...] = jnp.ones_like(l_scr)
        acc_scr[...] = jnp.broadcast_to(vn_ref[0].astype(bf16).astype(f32), (nrow, D_MODEL))

    s = jnp.concatenate([_dot(qbd_scr[...], k_refs[u][0].astype(bf16)) for u in range(G)], axis=1)
    tok = lax.broadcasted_iota(jnp.int32, s.shape, 1)
    dist = ((n_pages - step * G) * PAGE_SIZE - tok).astype(f32)
    s = s - slope_col * dist
    m = m_scr[...]
    m_new = jnp.maximum(m, jnp.max(s, -1, keepdims=True))
    p32 = jnp.exp(s - m_new)
    alpha = jnp.exp(m - m_new)
    l = alpha * l_scr[...] + jnp.sum(p32, -1, keepdims=True)
    p = p32.astype(bf16)
    acc = alpha * acc_scr[...]
    for u in range(G):
        vb = jnp.concatenate([v_refs[u][0, pl.ds(hh, PAGE_SIZE, stride=H_B), :] for hh in range(H_B)],
                             axis=1).astype(bf16)
        acc = acc + _dot(p[:, u * PAGE_SIZE:(u + 1) * PAGE_SIZE], vb)
    m_scr[...] = m_new
    l_scr[...] = l
    acc_scr[...] = acc

    @pl.when(step == pl.num_programs(1) - 1)
    def _():
        o = acc / l
        for hh in range(H_B):
            blk = o[2 * hh:2 * hh + 2, hh * LANES:(hh + 1) * LANES]
            d = blk[0:1, :] - lam_ref[0] * blk[1:2, :]
            d = d * lax.rsqrt(jnp.mean(d * d, -1, keepdims=True) + LN_EPS) * g_ref[...]
            o_ref[0, :, hh * LANES:(hh + 1) * LANES] = (d * (1.0 - lam_init)).astype(o_ref.dtype)


def _attn_paged(q, k_new, v_new, cache_kt, cache_v, page_table, lam, subln, lam_init):
    B = q.shape[0]
    G = PAGES_PER_STEP
    n_pages = page_table.shape[1]
    slopes = jnp.exp2(-8.0 * jnp.arange(1, H_B + 1, dtype=f32) / H_B)
    smem = pl.BlockSpec(memory_space=pltpu.SMEM)
    vec = pl.BlockSpec((1, 1, D_MODEL), lambda b, s, pt: (b, 0, 0))

    def page_index(u, ndim):
        return lambda b, s, pt: (pt[b * n_pages + s * G + u],) + (0,) * (ndim - 1)

    k_specs = [pl.BlockSpec((1, D_MODEL, PAGE_SIZE), page_index(u, 3)) for u in range(G)]
    v_specs = [pl.BlockSpec((1, PAGE_SIZE * H_B, LANES), page_index(u, 3)) for u in range(G)]
    grid_spec = pltpu.PrefetchScalarGridSpec(
        num_scalar_prefetch=1,
        grid=(B, n_pages // G),
        in_specs=[smem, smem, vec, vec, vec, pl.BlockSpec((1, LANES), lambda b, s, pt: (0, 0))] + k_specs + v_specs,
        out_specs=vec,
        scratch_shapes=[pltpu.VMEM((2 * H_B, D_MODEL), bf16),
                        pltpu.VMEM((2 * H_B, 1), f32),
                        pltpu.VMEM((2 * H_B, 1), f32),
                        pltpu.VMEM((2 * H_B, D_MODEL), f32)],
    )
    d = pl.pallas_call(
        functools.partial(_attn_paged_kernel, lam_init, n_pages),
        grid_spec=grid_spec,
        out_shape=jax.ShapeDtypeStruct((B, 1, D_MODEL), bf16),
        compiler_params=_cparams(("parallel", "arbitrary")),
        name="attn_paged",
    )(page_table.reshape(-1), slopes, lam.reshape(1), q.reshape(B, 1, -1), k_new.reshape(B, 1, -1),
      v_new.reshape(B, 1, -1), subln.reshape(1, -1), *([cache_kt] * G),
      *([cache_v.reshape(-1, PAGE_SIZE * H_B, LANES)] * G))
    return d.reshape(B, D_MODEL)


def _mix0(o, gs, x2d, W, row0):
    return _mix_out(o, (gs, W['gnorm']), x2d, W['w_o_a'], W['ln_mix_g'][0], W['ln_mix_b'][0],
                    W['w_router'][0], W['b_router'][0], row0)


def _mix1(d, x2, W):
    return _mix_out(d, None, x2, W['w_o_b'], W['ln_mix_g'][1], W['ln_mix_b'][1],
                    W['w_router'][1], W['b_router'][1])


def _moe(x, logits, layer, W, moe_rows):
    return _moe_ffn(x, logits, layer, W['w1'], W['b1'], W['w2'], W['b2'], moe_rows)


PROMPT_CHUNKS = 2


def _trunk_prompt(x, W, lam, subln, lam_init):
    B, S, _ = x.shape
    bc = B // PROMPT_CHUNKS
    tc = bc * S
    cs = range(PROMPT_CHUNKS)
    x2d = x.reshape(B * S, D_MODEL)
    pj = [_inproj(x2d, W['w_in'], W['lb'], c * tc, tc) for c in cs]
    hg = [_hgrn_seq(*pj[c][:4], bc, S) for c in cs]
    m0 = [_mix0(hg[c][0], pj[c][4], x2d, W, c * tc) for c in cs]
    e0 = [_moe(m0[c][0], m0[c][1], 0, W, MOE_ROWS) for c in cs]
    f0 = [_ffn_out_seq(m0[c][0], e0[c][0], e0[c][1], W['ln_ffn_g'][0], W['ln_ffn_b'][0], W, bc, S) for c in cs]
    at = [_attn_seq(f0[c][5], f0[c][3], f0[c][4], lam, subln, lam_init, bc, S) for c in cs]
    m1 = [_mix1(at[c], f0[c][0], W) for c in cs]
    e1 = [_moe(m1[c][0], m1[c][1], 1, W, MOE_ROWS) for c in cs]
    x4 = [_ffn_out(m1[c][0], e1[c][0], e1[c][1], W['ln_ffn_g'][1], W['ln_ffn_b'][1], [])[0] for c in cs]
    y = jnp.concatenate(x4, axis=0).reshape(B, S, D_MODEL)
    state = jnp.concatenate([hg[c][1] for c in cs], axis=0)
    kt = jnp.concatenate([f0[c][1] for c in cs], axis=0)
    v3 = jnp.concatenate([f0[c][2] for c in cs], axis=0)
    k_out = jnp.transpose(kt.reshape(B, H_B, 2, HD_B, S), (0, 4, 1, 2, 3))
    return y, state[None], k_out, v3.reshape(B, S, H_B, 2 * HD_B)


def _trunk_sample(x, state0, cache_kt, cache_v, page_table, W, lam, subln, lam_init):
    B = x.shape[0]
    x2d = x.reshape(B, D_MODEL)
    qs, k, lf, v, gs = _inproj(x2d, W['w_in'], W['lb'], 0, B)
    o, state = _hgrn_step(state0, qs, k, lf, v)
    x1, logits = _mix0(o, gs, x2d, W, 0)
    yk, gates = _moe(x1, logits, 0, W, 32)
    x2, k_sh, v_sh, q = _ffn_out(x1, yk, gates, W['ln_ffn_g'][0], W['ln_ffn_b'][0],
                                 [W['w_k'], W['w_v'], W['w_q']])
    d = _attn_paged(q, k_sh, v_sh, cache_kt, cache_v, page_table, lam, subln, lam_init)
    x3, logits = _mix1(d, x2, W)
    yk, gates = _moe(x3, logits, 1, W, 32)
    (x4,) = _ffn_out(x3, yk, gates, W['ln_ffn_g'][1], W['ln_ffn_b'][1], [])
    return (x4.reshape(B, 1, D_MODEL), state[None],
            k_sh.reshape(B, 1, H_B, 2, HD_B), v_sh.reshape(B, 1, H_B, 2 * HD_B))


def kernel(x_prompt, x_sample, state_hgrn, cache_k, cache_v, page_table, w_in_a, lb_logits, gnorm_a, w_o_a,
           w_kv, w_q_b, lam_q1, lam_k1, lam_q2, lam_k2, subln_b, w_o_b, ln_mix_g, ln_mix_b, ln_ffn_g, ln_ffn_b,
           w_router, b_router, w1, b1, w2, b2):
    q_dim = H_B * 2 * HD_B
    lb = jnp.cumsum(jax.nn.softmax(lb_logits.astype(f32), axis=0), axis=0)[:N_A_LAYERS]
    W = dict(
        lb=lb[0].reshape(1, D_A), w_in=w_in_a[0].astype(bf16), gnorm=gnorm_a[0], w_o_a=w_o_a[0].astype(bf16),
        w_k=w_kv[:, :q_dim].astype(bf16), w_v=w_kv[:, q_dim:].astype(bf16), w_q=w_q_b[0].astype(bf16),
        w_q_scaled=(w_q_b[0] * (HD_B ** -0.5)).astype(bf16),
        w_o_b=w_o_b[0].astype(bf16), ln_mix_g=ln_mix_g, ln_mix_b=ln_mix_b, ln_ffn_g=ln_ffn_g, ln_ffn_b=ln_ffn_b,
        w_router=w_router, b_router=b_router, w1=w1, b1=b1, w2=w2, b2=b2)
    layer = N_A_LAYERS
    lam_init = 0.8 - 0.6 * math.exp(-0.3 * layer)
    lam = (jnp.exp(jnp.sum((lam_q1[0] * lam_k1[0]).astype(f32)))
           - jnp.exp(jnp.sum((lam_q2[0] * lam_k2[0]).astype(f32))) + lam_init)
    subln = subln_b[0]

    y_p, st_p, k_p, v_p = _trunk_prompt(x_prompt, W, lam, subln, lam_init)

    n_pool = cache_k.shape[0]
    cache_kt = jnp.transpose(cache_k, (0, 2, 3, 4, 1)).reshape(n_pool, D_MODEL, PAGE_SIZE)
    y_s, st_s, k_s, v_s = _trunk_sample(x_sample, state_hgrn[0], cache_kt, cache_v, page_table, W,
                                        lam, subln, lam_init)
    return (y_p, y_s, st_p, st_s, k_p, v_p, k_s, v_s)
```

```python
import functools
import math

import numpy as np
import jax
import jax.numpy as jnp
from jax import lax
from jax.experimental import pallas as pl
from jax.experimental.pallas import tpu as pltpu

f32 = jnp.float32
bf16 = jnp.bfloat16

D_MODEL = 1024
DEPTH = 2
N_A_LAYERS = 1
H_A = 8
DK_A = 128
DV_A = 128
D_A = H_A * DK_A
HD_B = 64
H_B = 8
N_EXPERTS = 32
TOP_K = 4
D_FF = D_MODEL
SWIGLU_LIMIT = 7.0
SWIGLU_ALPHA = 1.702
DEEPNORM_ALPHA = (2.0 * DEPTH) ** 0.25
LN_EPS = 1e-5
NEG_INF = -1e30
PAGE_SIZE = 128
LOG2_E = 1.4426950408889634

LANES = 128
VMEM_LIMIT = 56 * 1024 * 1024
MOE_VMEM_LIMIT = 60 * 1024 * 1024

HGRN_TILE = 128
HGRN_HEADS_PER_STEP = 2
MOE_ROWS = 512


def _cparams(sem):
    return pltpu.CompilerParams(dimension_semantics=sem, vmem_limit_bytes=VMEM_LIMIT)


def _sigmoid(x):
    return 1.0 / (1.0 + jnp.exp(-x))


def _layer_norm_rows(x, g, b):
    mu = jnp.mean(x, -1, keepdims=True)
    xc = x - mu
    var = jnp.mean(xc * xc, -1, keepdims=True)
    return xc * lax.rsqrt(var + LN_EPS) * g + b


def _dot(a, b):
    return jnp.dot(a, b, preferred_element_type=f32)


def _dot_nt(a, b):
    return lax.dot_general(a, b, (((1,), (1,)), ((), ())), preferred_element_type=f32)


def _dot_tn(a, b):
    return lax.dot_general(a, b, (((0,), (0,)), ((), ())), preferred_element_type=f32)


def _inproj_kernel(x_ref, w_ref, lb_ref, qs_ref, k_ref, lf_ref, v_ref, gs_ref):
    h = _dot(x_ref[...].astype(bf16), w_ref[...])
    q = h[:, :D_A]
    fr = h[:, D_A:2 * D_A]
    lb = lb_ref[...]
    z = jnp.exp(-jnp.abs(fr))
    r = 1.0 / (1.0 + z)
    zr = z * r
    pos = fr >= 0
    sig = jnp.where(pos, r, zr)
    nsig = jnp.where(pos, zr, r)
    qs_ref[...] = q * _sigmoid(q)
    k_ref[...] = (1.0 - lb) * nsig
    lf_ref[...] = jnp.log(lb + (1.0 - lb) * sig)
    v_ref[...] = h[:, 2 * D_A:3 * D_A]
    g = h[:, 3 * D_A:]
    gs_ref[...] = g * _sigmoid(g)


def _inproj(x2d, w_in, lb, row0, T):
    tm = min(256, T)
    blk0 = row0 // tm
    out = jax.ShapeDtypeStruct((T, D_A), f32)
    row = pl.BlockSpec((tm, D_A), lambda i: (i, 0))
    return pl.pallas_call(
        _inproj_kernel,
        grid=(T // tm,),
        in_specs=[pl.BlockSpec((tm, D_MODEL), lambda i: (i + blk0, 0)),
                  pl.BlockSpec((D_MODEL, 4 * D_A), lambda i: (0, 0)),
                  pl.BlockSpec((1, D_A), lambda i: (0, 0))],
        out_specs=[row] * 5,
        out_shape=[out] * 5,
        compiler_params=_cparams(("parallel",)),
        name="hgrn_inproj",
    )(x2d, w_in, lb)


def _hgrn_levels(R):
    n = R // 2
    out = []
    while n >= 1:
        out.append(n)
        n //= 2
    return out


def _hgrn_masks(R):
    t = np.arange(R)[:, None]
    s = np.arange(R)[None, :]
    ms = []
    for n in _hgrn_levels(R):
        ms.append((t // (2 * n) == s // (2 * n)) & (t % (2 * n) >= n) & (s % (2 * n) < n))
    ms.append(t == s)
    return np.stack(ms).astype(np.float32)


def _level_ref_rows(b, n):
    R = b.shape[0]
    if 2 * n >= 8:
        b3 = b.reshape(R // (2 * n), 2 * n, LANES)
        return jnp.broadcast_to(b3[:, n - 1:n, :], b3.shape).reshape(R, LANES)
    b3 = b.reshape(R // 8, 8, LANES)
    sub = lax.broadcasted_iota(jnp.int32, b3.shape, 1)
    out = None
    for r0 in range(0, 8, 2 * n):
        row = jnp.broadcast_to(b3[:, r0 + n - 1:r0 + n, :], b3.shape)
        out = row if out is None else jnp.where(sub >= r0, row, out)
    return out.reshape(R, LANES)


def _split3(x):
    hi = x.astype(bf16)
    r1 = x - hi.astype(f32)
    mid = r1.astype(bf16)
    lo = (r1 - mid.astype(f32)).astype(bf16)
    return hi, mid, lo


def _hgrn_seq_kernel(qs_ref, k_ref, lf_ref, v_ref, masks_ref, tril_ref, o_ref, st_ref, state_t):
    R = HGRN_TILE
    levels = _hgrn_levels(R)
    step = pl.program_id(2)

    @pl.when(step == 0)
    def _():
        state_t[...] = jnp.zeros_like(state_t)

    def tile_head(r, hh):
        cols = slice(hh * LANES, (hh + 1) * LANES)
        q = qs_ref[pl.ds(r, R), cols]
        k = k_ref[pl.ds(r, R), cols]
        v = v_ref[pl.ds(r, R), cols].astype(bf16)
        hi, mid, lo = _split3(lf_ref[pl.ds(r, R), cols] * LOG2_E)
        c3 = _dot(tril_ref[...], jnp.concatenate([hi, mid, lo], axis=1))
        b = c3[:, :LANES] + c3[:, LANES:2 * LANES] + c3[:, 2 * LANES:]
        a = masks_ref[len(levels)] * _dot_nt(q.astype(bf16), k.astype(bf16))
        for li, n in enumerate(levels):
            e = jnp.exp2(-jnp.abs(b - _level_ref_rows(b, n)))
            a = a + masks_ref[li] * _dot_nt((q * e).astype(bf16), (k * e).astype(bf16))
        st = state_t[hh]
        b_last = b[R - 1:R, :]
        qd = (q * jnp.exp2(b)).astype(bf16)
        kd = (k * jnp.exp2(b_last - b)).astype(bf16)
        o_ref[pl.ds(r, R), cols] = _dot(a.astype(bf16), v) + _dot_nt(qd, st.astype(bf16))
        state_t[hh] = st * jnp.exp2(b_last) + _dot_tn(v, kd)

    def tile(i, carry):
        r = pl.multiple_of(i * R, R)
        for hh in range(HGRN_HEADS_PER_STEP):
            tile_head(r, hh)
        return carry

    lax.fori_loop(0, qs_ref.shape[0] // R, tile, 0)

    @pl.when(step == pl.num_programs(2) - 1)
    def _():
        st_ref[0] = state_t[...]


def _hgrn_seq(qs, k, lf, v, B, S):
    R = HGRN_TILE
    G = HGRN_HEADS_PER_STEP
    rb = min(512, S)
    nst = S // rb
    masks = jnp.asarray(_hgrn_masks(R))
    tril = jnp.asarray(np.tril(np.ones((R, R), np.float32)), dtype=bf16)
    row = pl.BlockSpec((rb, G * LANES), lambda b, h, j: (b * nst + j, h))
    o, st_t = pl.pallas_call(
        _hgrn_seq_kernel,
        grid=(B, H_A // G, nst),
        in_specs=[row, row, row, row,
                  pl.BlockSpec(masks.shape, lambda b, h, j: (0, 0, 0)),
                  pl.BlockSpec((R, R), lambda b, h, j: (0, 0))],
        out_specs=[row, pl.BlockSpec((1, G, DV_A, DK_A), lambda b, h, j: (b, h, 0, 0))],
        out_shape=[jax.ShapeDtypeStruct((B * S, D_A), f32),
                   jax.ShapeDtypeStruct((B, H_A, DV_A, DK_A), f32)],
        scratch_shapes=[pltpu.VMEM((G, DV_A, DK_A), f32)],
        compiler_params=_cparams(("parallel", "parallel", "arbitrary")),
        name="hgrn_seq",
    )(qs, k, lf, v, masks, tril)
    return o, jnp.swapaxes(st_t, 2, 3)


def _hgrn_step_kernel(s_ref, q_ref, k_ref, lf_ref, v_ref, o_ref, so_ref):
    for h in range(H_A):
        s_new = jnp.exp(lf_ref[0, h]) * s_ref[0, h] + k_ref[0, h] * v_ref[0, h]
        so_ref[0, h] = s_new
        o_ref[0, h] = jnp.sum(q_ref[0, h] * s_new, axis=0, keepdims=True)


def _hgrn_step(state, qs, k, lf, v):
    B = state.shape[0]
    col = lambda t: t.reshape(B, H_A, DK_A, 1)
    cspec = pl.BlockSpec((1, H_A, DK_A, 1), lambda b: (b, 0, 0, 0))
    rspec = pl.BlockSpec((1, H_A, 1, DV_A), lambda b: (b, 0, 0, 0))
    sspec = pl.BlockSpec((1, H_A, DK_A, DV_A), lambda b: (b, 0, 0, 0))
    o, s_new = pl.pallas_call(
        _hgrn_step_kernel,
        grid=(B,),
        in_specs=[sspec, cspec, cspec, cspec, rspec],
        out_specs=[rspec, sspec],
        out_shape=[jax.ShapeDtypeStruct((B, H_A, 1, DV_A), f32),
                   jax.ShapeDtypeStruct(state.shape, f32)],
        compiler_params=_cparams(("parallel",)),
        name="hgrn_step",
    )(state, col(qs), col(k), col(lf), v.reshape(B, H_A, 1, DV_A))
    return o.reshape(B, D_A), s_new


def _mix_out_kernel(gated, *refs):
    if gated:
        m_ref, gs_ref, gn_ref, x_ref, wo_ref, lg_ref, lbias_ref, wr_ref, br_ref, x1_ref, idx_ref, gate_ref = refs
        m = m_ref[...] * gs_ref[...]
        m = m * lax.rsqrt(jnp.mean(m * m, -1, keepdims=True) + LN_EPS) * gn_ref[...]
        m = m.astype(bf16)
    else:
        m_ref, x_ref, wo_ref, lg_ref, lbias_ref, wr_ref, br_ref, x1_ref, idx_ref, gate_ref = refs
        m = m_ref[...]
    mix = _dot(m, wo_ref[...])
    x1 = _layer_norm_rows(DEEPNORM_ALPHA * x_ref[...] + mix, lg_ref[...], lbias_ref[...])
    x1_ref[...] = x1
    logits = jnp.dot(x1, wr_ref[...], preferred_element_type=f32,
                     precision=lax.Precision.HIGHEST) + br_ref[...]
    lane = lax.broadcasted_iota(jnp.int32, logits.shape, 1).astype(f32)
    cur = logits
    idx_out = jnp.zeros_like(logits)
    gate_out = jnp.zeros_like(logits)
    denom = None
    top = None
    for kk in range(TOP_K):
        v = jnp.max(cur, -1, keepdims=True)
        idx = jnp.min(jnp.where(cur == v, lane, float(N_EXPERTS)), -1, keepdims=True)
        top = v if top is None else top
        e = jnp.exp(v - top)
        denom = e if denom is None else denom + e
        idx_out = jnp.where(lane == kk, idx, idx_out)
        gate_out = jnp.where(lane == kk, e, gate_out)
        cur = jnp.where(lane == idx, -jnp.inf, cur)
    idx_ref[...] = idx_out.astype(jnp.int32)
    gate_ref[...] = gate_out / denom


def _mix_out(m, gate, x2d, w_o, ln_g, ln_b, w_router, b_router, row0=0):
    T = m.shape[0]
    tm = min(256, T)
    blk0 = row0 // tm
    row = pl.BlockSpec((tm, D_MODEL), lambda i: (i, 0))
    vec = pl.BlockSpec((1, D_MODEL), lambda i: (0, 0))
    gated = gate is not None
    ins = [m] + ([gate[0], gate[1].reshape(1, -1)] if gated else []) + [
        x2d, w_o, ln_g.reshape(1, -1), ln_b.reshape(1, -1), w_router, b_router.reshape(1, -1)]
    specs = [row] + ([row, vec] if gated else []) + [
        pl.BlockSpec((tm, D_MODEL), lambda i: (i + blk0, 0)),
        pl.BlockSpec((D_MODEL, D_MODEL), lambda i: (0, 0)), vec, vec,
        pl.BlockSpec((D_MODEL, N_EXPERTS), lambda i: (0, 0)),
        pl.BlockSpec((1, N_EXPERTS), lambda i: (0, 0))]
    return pl.pallas_call(
        functools.partial(_mix_out_kernel, gated),
        grid=(T // tm,),
        in_specs=specs,
        out_specs=[row] + [pl.BlockSpec((tm, N_EXPERTS), lambda i: (i, 0))] * 2,
        out_shape=[jax.ShapeDtypeStruct((T, D_MODEL), f32),
                   jax.ShapeDtypeStruct((T, N_EXPERTS), jnp.int32),
                   jax.ShapeDtypeStruct((T, N_EXPERTS), f32)],
        compiler_params=_cparams(("parallel",)),
        name="mix_out_gated" if gated else "mix_out",
    )(*ins)


MOE_FF_CHUNK = 512
GATHER_SRC_ROWS = 40960


def _moe_kernel(blk_e_ref, n_used_ref, x_ref, w1_ref, b1_ref, w2_ref, b2_ref, y_ref, w1b, w2b):
    i = pl.program_id(0)
    live = i < n_used_ref[0]
    prev = blk_e_ref[jnp.maximum(i - 1, 0)]

    @pl.when(live & ((i == 0) | (blk_e_ref[i] != prev)))
    def _():
        w1b[...] = w1_ref[0, 0].astype(bf16)
        w2b[...] = w2_ref[0, 0].astype(bf16)

    @pl.when(live)
    def _():
        x = x_ref[...]
        y = None
        for c in range(0, D_FF, MOE_FF_CHUNK):
            hg = _dot(x, w1b[:, c:c + MOE_FF_CHUNK]) + b1_ref[0, 0, :, c:c + MOE_FF_CHUNK]
            hl = (_dot(x, w1b[:, D_FF + c:D_FF + c + MOE_FF_CHUNK])
                  + b1_ref[0, 0, :, D_FF + c:D_FF + c + MOE_FF_CHUNK])
            x_glu = jnp.minimum(hg, SWIGLU_LIMIT)
            x_lin = jnp.clip(hl, -SWIGLU_LIMIT, SWIGLU_LIMIT)
            act = x_glu * _sigmoid(SWIGLU_ALPHA * x_glu) * (x_lin + 1.0)
            part = _dot(act.astype(bf16), w2b[c:c + MOE_FF_CHUNK, :])
            y = part if y is None else y + part
        y_ref[...] = (y + b2_ref[0, 0]).astype(y_ref.dtype)


def _moe_rows(x_rows, blk_e, n_used, layer, w1, b1, w2, b2, bm):
    nb = x_rows.shape[0] // bm
    grid_spec = pltpu.PrefetchScalarGridSpec(
        num_scalar_prefetch=2,
        grid=(nb,),
        in_specs=[pl.BlockSpec((bm, D_MODEL), lambda i, e, n: (i, 0)),
                  pl.BlockSpec((1, 1, D_MODEL, 2 * D_FF), lambda i, e, n: (layer, e[i], 0, 0)),
                  pl.BlockSpec((1, 1, 1, 2 * D_FF), lambda i, e, n: (layer, e[i], 0, 0)),
                  pl.BlockSpec((1, 1, D_FF, D_MODEL), lambda i, e, n: (layer, e[i], 0, 0)),
                  pl.BlockSpec((1, 1, 1, D_MODEL), lambda i, e, n: (layer, e[i], 0, 0))],
        out_specs=pl.BlockSpec((bm, D_MODEL), lambda i, e, n: (i, 0)),
        scratch_shapes=[pltpu.VMEM((D_MODEL, 2 * D_FF), bf16), pltpu.VMEM((D_FF, D_MODEL), bf16)],
    )
    return pl.pallas_call(
        _moe_kernel,
        grid_spec=grid_spec,
        out_shape=jax.ShapeDtypeStruct((nb * bm, D_MODEL), bf16),
        compiler_params=pltpu.CompilerParams(dimension_semantics=("arbitrary",),
                                             vmem_limit_bytes=MOE_VMEM_LIMIT),
        name="moe_experts",
    )(blk_e, n_used, x_rows, w1, b1.reshape(DEPTH, N_EXPERTS, 1, -1), w2, b2.reshape(DEPTH, N_EXPERTS, 1, -1))


def _moe_ffn(x1, route_idx, route_gate, layer, w1, b1, w2, b2, bm):
    T = x1.shape[0]
    n_slots = T * TOP_K
    top_idx = route_idx[:, :TOP_K]
    gates = route_gate[:, :TOP_K]
    flat_e = top_idx.reshape(-1)
    order = jnp.argsort(flat_e)
    inv = jnp.argsort(order)
    counts = jnp.bincount(flat_e, length=N_EXPERTS)
    padded = (counts + bm - 1) // bm * bm
    pad_end = jnp.cumsum(padded)
    pad_start = pad_end - padded
    start = jnp.cumsum(counts) - counts
    n_blocks = (n_slots + N_EXPERTS * (bm - 1) + bm - 1) // bm
    blk_e = jnp.minimum(jnp.sum(pad_end[None, :] <= (jnp.arange(n_blocks) * bm)[:, None], axis=1),
                        N_EXPERTS - 1).astype(jnp.int32)
    n_used = (pad_end[-1] // bm).astype(jnp.int32).reshape(1)
    row = jnp.arange(n_blocks * bm)
    row_e = jnp.repeat(blk_e, bm)
    j = row - pad_start[row_e]
    src = jnp.clip(start[row_e] + j, 0, n_slots - 1)
    row_tok = jnp.where((j < counts[row_e]) & (row < pad_end[-1]), order[src] // TOP_K, T).astype(jnp.int32)
    n_zero = max(1, GATHER_SRC_ROWS - T) if T >= GATHER_SRC_ROWS // 4 else 1
    x_pad = jnp.pad(x1.astype(bf16), ((0, n_zero), (0, 0)))
    y_rows = _moe_rows(x_pad[row_tok], blk_e, n_used, layer, w1, b1, w2, b2, bm)
    pos = (pad_start[flat_e] + inv - start[flat_e]).astype(jnp.int32)
    pos_k = pos.reshape(T, TOP_K).T.reshape(-1)
    return y_rows[pos_k].reshape(TOP_K, T, D_MODEL), gates


def _ffn_combine(x_ref, y_refs, gt_ref, g_ref, b_ref):
    gt = gt_ref[...]
    y = y_refs[0][0].astype(f32) * gt[:, 0:1]
    for kk in range(1, TOP_K):
        y = y + y_refs[kk][0].astype(f32) * gt[:, kk:kk + 1]
    return _layer_norm_rows(DEEPNORM_ALPHA * x_ref[...] + y, g_ref[...], b_ref[...])


def _ffn_out_kernel(n_proj, x_ref, y0, y1, y2, y3, gt_ref, g_ref, b_ref, *rest):
    x2 = _ffn_combine(x_ref, (y0, y1, y2, y3), gt_ref, g_ref, b_ref)
    rest[n_proj][...] = x2
    xb = x2.astype(bf16)
    for p in range(n_proj):
        rest[n_proj + 1 + p][...] = _dot(xb, rest[p][...])


def _ffn_in_specs(tm, row_index):
    row = pl.BlockSpec((tm, D_MODEL), lambda *g: (row_index(*g), 0))
    yk = [pl.BlockSpec((1, tm, D_MODEL), functools.partial(lambda kk, *g: (kk, row_index(*g), 0), kk))
          for kk in range(TOP_K)]
    vec = pl.BlockSpec((1, D_MODEL), lambda *g: (0, 0))
    return row, [row] + yk + [pl.BlockSpec((tm, TOP_K), lambda *g: (row_index(*g), 0)), vec, vec]


def _ffn_out(x1, yk, gates, ln_g, ln_b, proj_ws):
    T = x1.shape[0]
    tm = min(256, T)
    row, specs = _ffn_in_specs(tm, lambda i: i)
    n_proj = len(proj_ws)
    return pl.pallas_call(
        functools.partial(_ffn_out_kernel, n_proj),
        grid=(T // tm,),
        in_specs=specs + [pl.BlockSpec(w.shape, lambda i: (0, 0)) for w in proj_ws],
        out_specs=[row] + [pl.BlockSpec((tm, w.shape[1]), lambda i: (i, 0)) for w in proj_ws],
        out_shape=[jax.ShapeDtypeStruct((T, D_MODEL), f32)] +
                  [jax.ShapeDtypeStruct((T, w.shape[1]), f32) for w in proj_ws],
        compiler_params=_cparams(("parallel",)),
        name="ffn_out_%d" % n_proj,
    )(x1, yk, yk, yk, yk, gates, ln_g.reshape(1, -1), ln_b.reshape(1, -1), *proj_ws)


def _ffn_out_seq_kernel(x_ref, y0, y1, y2, y3, gt_ref, g_ref, b_ref, wk_ref, wkt_ref, wv_ref, wvt_ref, wqt_ref,
                        x2_ref, kt_ref, v3_ref, kb_ref, vt_ref, qt_ref):
    x2 = _ffn_combine(x_ref, (y0, y1, y2, y3), gt_ref, g_ref, b_ref)
    x2_ref[...] = x2
    xb = x2.astype(bf16)
    kb_ref[...] = _dot(xb, wk_ref[...]).astype(bf16)
    kt_ref[0] = _dot_nt(wkt_ref[...], xb)
    vt_ref[0] = _dot_nt(wvt_ref[...], xb).astype(bf16)
    qt_ref[0] = _dot_nt(wqt_ref[...], xb).astype(bf16)
    v = _dot(xb, wv_ref[...])
    for h in range(H_B):
        v3_ref[:, h, :] = v[:, h * LANES:(h + 1) * LANES]


def _ffn_out_seq(x1, yk, gates, ln_g, ln_b, W, B, S):
    T = B * S
    tm = 256
    nj = S // tm
    row, specs = _ffn_in_specs(tm, lambda b, j: b * nj + j)
    wspec = pl.BlockSpec((D_MODEL, D_MODEL), lambda b, j: (0, 0), pipeline_mode=pl.Buffered(1))
    tspec = pl.BlockSpec((1, D_MODEL, tm), lambda b, j: (b, 0, j))
    return pl.pallas_call(
        _ffn_out_seq_kernel,
        grid=(B, nj),
        in_specs=specs + [wspec] * 5,
        out_specs=[row, tspec, pl.BlockSpec((tm, H_B, LANES), lambda b, j: (b * nj + j, 0, 0)), row, tspec, tspec],
        out_shape=[jax.ShapeDtypeStruct((T, D_MODEL), f32),
                   jax.ShapeDtypeStruct((B, D_MODEL, S), f32),
                   jax.ShapeDtypeStruct((T, H_B, LANES), f32),
                   jax.ShapeDtypeStruct((T, D_MODEL), bf16),
                   jax.ShapeDtypeStruct((B, D_MODEL, S), bf16),
                   jax.ShapeDtypeStruct((B, D_MODEL, S), bf16)],
        compiler_params=_cparams(("parallel", "parallel")),
        name="ffn_out_seq",
    )(x1, yk, yk, yk, yk, gates, ln_g.reshape(1, -1), ln_b.reshape(1, -1),
      W['w_k'], W['w_k'].T, W['w_v'], W['w_v'].T, W['w_q_scaled'].T)


ATT_T = 512


def _attn_seq_kernel(lam_init, slopes_ref, lam_ref, qt_ref, k_ref, vt_ref, g_ref, o_ref,
                     qt2_scr, bias_scr, bias_diag_scr, acc_scr):
    t = ATT_T
    h = pl.program_id(1)
    i = pl.program_id(2)
    slope = slopes_ref[h]

    @pl.when(i == 0)
    def _():
        kk = lax.broadcasted_iota(jnp.int32, (t, 2 * t), 0)
        qq = lax.broadcasted_iota(jnp.int32, (t, 2 * t), 1)
        qq = jnp.where(qq >= t, qq - t, qq)
        rel = slope * (kk - qq).astype(f32)
        bias_scr[...] = rel
        bias_diag_scr[...] = jnp.where(kk <= qq, rel, NEG_INF)

    qt = qt_ref[0]
    dim = lax.broadcasted_iota(jnp.int32, qt.shape, 0)
    zero = jnp.zeros_like(qt)
    qt2_scr[:, :t] = jnp.where(dim < HD_B, qt, zero)
    qt2_scr[:, t:] = jnp.where(dim >= HD_B, qt, zero)
    acc_scr[...] = jnp.zeros_like(acc_scr)

    def kv_block(j, bias_ref, m_old, l_old):
        c = pl.multiple_of(j * t, t)
        s = _dot(k_ref[pl.ds(c, t), :], qt2_scr[...]) + bias_ref[...]
        shift = slope * ((i - j) * t).astype(f32)
        m_new = jnp.maximum(m_old, jnp.max(s, 0, keepdims=True) - shift)
        p = jnp.exp(s - (m_new + shift))
        alpha = jnp.exp(m_old - m_new)
        l_new = alpha * l_old + jnp.sum(p, 0, keepdims=True)
        acc_scr[...] = alpha * acc_scr[...] + _dot(vt_ref[0, :, pl.ds(c, t)], p.astype(bf16))
        return m_new, l_new

    m0 = jnp.full((1, 2 * t), NEG_INF, f32)
    l0 = jnp.zeros((1, 2 * t), f32)
    m, l = lax.fori_loop(0, i, lambda j, ml: kv_block(j, bias_scr, *ml), (m0, l0))
    m, l = kv_block(i, bias_diag_scr, m, l)

    o = acc_scr[...] / l
    d = o[:, :t] - lam_ref[0] * o[:, t:]
    d = d * lax.rsqrt(jnp.mean(d * d, 0, keepdims=True) + LN_EPS) * g_ref[...]
    o_ref[...] = (d * (1.0 - lam_init)).T.astype(o_ref.dtype)


def _attn_seq(qt, kb, vt, lam, subln, lam_init, B, S):
    t = ATT_T
    nq = S // t
    slopes = jnp.exp2(-8.0 * jnp.arange(1, H_B + 1, dtype=f32) / H_B)
    smem = pl.BlockSpec(memory_space=pltpu.SMEM)
    return pl.pallas_call(
        functools.partial(_attn_seq_kernel, lam_init),
        grid=(B, H_B, nq),
        in_specs=[smem, smem,
                  pl.BlockSpec((1, LANES, t), lambda b, h, i: (b, h, i)),
                  pl.BlockSpec((S, LANES), lambda b, h, i: (b, h)),
                  pl.BlockSpec((1, LANES, S), lambda b, h, i: (b, h, 0)),
                  pl.BlockSpec((LANES, 1), lambda b, h, i: (0, 0))],
        out_specs=pl.BlockSpec((t, LANES), lambda b, h, i: (b * nq + i, h)),
        out_shape=jax.ShapeDtypeStruct((B * S, D_MODEL), bf16),
        scratch_shapes=[pltpu.VMEM((LANES, 2 * t), bf16),
                        pltpu.VMEM((t, 2 * t), f32),
                        pltpu.VMEM((t, 2 * t), f32),
                        pltpu.VMEM((LANES, 2 * t), f32)],
        compiler_params=_cparams(("parallel", "arbitrary", "arbitrary")),
        name="attn_seq",
    )(slopes, lam.reshape(1), qt, kb, vt, subln.reshape(-1, 1))


PAGES_PER_STEP = 8


def _attn_paged_kernel(lam_init, n_pages, pt_ref, slopes_ref, lam_ref, q_ref, kn_ref, vn_ref, g_ref, *rest):
    G = PAGES_PER_STEP
    k_refs = rest[:G]
    v_refs = rest[G:2 * G]
    o_ref, qbd_scr, m_scr, l_scr, acc_scr = rest[2 * G:]
    step = pl.program_id(1)
    nrow = 2 * H_B
    r1 = lax.broadcasted_iota(jnp.int32, (nrow, 1), 0)
    slope_col = jnp.zeros((nrow, 1), f32)
    for hh in range(H_B):
        slope_col = jnp.where(r1 // 2 == hh, slopes_ref[hh], slope_col)

    @pl.when(step == 0)
    def _():
        rowi = lax.broadcasted_iota(jnp.int32, (nrow, D_MODEL), 0)
        coli = lax.broadcasted_iota(jnp.int32, (nrow, D_MODEL), 1)
        own = (coli // HD_B) == rowi
        q = q_ref[0] * (HD_B ** -0.5)
        qbd = jnp.where(own, jnp.broadcast_to(q, (nrow, D_MODEL)), 0.0).astype(bf16)
        qbd_scr[...] = qbd
        m_scr[...] = jnp.sum(qbd.astype(f32) * kn_ref[0].astype(bf16).astype(f32), -1, keepdims=True)
        l_scr[...] = jnp.ones_like(l_scr)
        acc_scr[...] = jnp.broadcast_to(vn_ref[0].astype(bf16).astype(f32), (nrow, D_MODEL))

    s = jnp.concatenate([_dot(qbd_scr[...], k_refs[u][0].astype(bf16)) for u in range(G)], axis=1)
    tok = lax.broadcasted_iota(jnp.int32, s.shape, 1)
    dist = ((n_pages - step * G) * PAGE_SIZE - tok).astype(f32)
    s = s - slope_col * dist
    m = m_scr[...]
    m_new = jnp.maximum(m, jnp.max(s, -1, keepdims=True))
    p32 = jnp.exp(s - m_new)
    alpha = jnp.exp(m - m_new)
    l = alpha * l_scr[...] + jnp.sum(p32, -1, keepdims=True)
    p = p32.astype(bf16)
    acc = alpha * acc_scr[...]
    for u in range(G):
        vb = jnp.concatenate([v_refs[u][0, pl.ds(hh, PAGE_SIZE, stride=H_B), :] for hh in range(H_B)],
                             axis=1).astype(bf16)
        acc = acc + _dot(p[:, u * PAGE_SIZE:(u + 1) * PAGE_SIZE], vb)
    m_scr[...] = m_new
    l_scr[...] = l
    acc_scr[...] = acc

    @pl.when(step == pl.num_programs(1) - 1)
    def _():
        o = acc / l
        for hh in range(H_B):
            blk = o[2 * hh:2 * hh + 2, hh * LANES:(hh + 1) * LANES]
            d = blk[0:1, :] - lam_ref[0] * blk[1:2, :]
            d = d * lax.rsqrt(jnp.mean(d * d, -1, keepdims=True) + LN_EPS) * g_ref[...]
            o_ref[0, :, hh * LANES:(hh + 1) * LANES] = (d * (1.0 - lam_init)).astype(o_ref.dtype)


def _attn_paged(q, k_new, v_new, cache_kt, cache_v, page_table, lam, subln, lam_init):
    B = q.shape[0]
    G = PAGES_PER_STEP
    n_pages = page_table.shape[1]
    slopes = jnp.exp2(-8.0 * jnp.arange(1, H_B + 1, dtype=f32) / H_B)
    smem = pl.BlockSpec(memory_space=pltpu.SMEM)
    vec = pl.BlockSpec((1, 1, D_MODEL), lambda b, s, pt: (b, 0, 0))

    def page_index(u, ndim):
        return lambda b, s, pt: (pt[b * n_pages + s * G + u],) + (0,) * (ndim - 1)

    k_specs = [pl.BlockSpec((1, D_MODEL, PAGE_SIZE), page_index(u, 3)) for u in range(G)]
    v_specs = [pl.BlockSpec((1, PAGE_SIZE * H_B, LANES), page_index(u, 3)) for u in range(G)]
    grid_spec = pltpu.PrefetchScalarGridSpec(
        num_scalar_prefetch=1,
        grid=(B, n_pages // G),
        in_specs=[smem, smem, vec, vec, vec, pl.BlockSpec((1, LANES), lambda b, s, pt: (0, 0))] + k_specs + v_specs,
        out_specs=vec,
        scratch_shapes=[pltpu.VMEM((2 * H_B, D_MODEL), bf16),
                        pltpu.VMEM((2 * H_B, 1), f32),
                        pltpu.VMEM((2 * H_B, 1), f32),
                        pltpu.VMEM((2 * H_B, D_MODEL), f32)],
    )
    d = pl.pallas_call(
        functools.partial(_attn_paged_kernel, lam_init, n_pages),
        grid_spec=grid_spec,
        out_shape=jax.ShapeDtypeStruct((B, 1, D_MODEL), bf16),
        compiler_params=_cparams(("parallel", "arbitrary")),
        name="attn_paged",
    )(page_table.reshape(-1), slopes, lam.reshape(1), q.reshape(B, 1, -1), k_new.reshape(B, 1, -1),
      v_new.reshape(B, 1, -1), subln.reshape(1, -1), *([cache_kt] * G),
      *([cache_v.reshape(-1, PAGE_SIZE * H_B, LANES)] * G))
    return d.reshape(B, D_MODEL)


def _mix0(o, gs, x2d, W, row0):
    return _mix_out(o, (gs, W['gnorm']), x2d, W['w_o_a'], W['ln_mix_g'][0], W['ln_mix_b'][0],
                    W['w_router'][0], W['b_router'][0], row0)


def _mix1(d, x2, W):
    return _mix_out(d, None, x2, W['w_o_b'], W['ln_mix_g'][1], W['ln_mix_b'][1],
                    W['w_router'][1], W['b_router'][1])


def _moe(x, route_idx, route_gate, layer, W, moe_rows):
    return _moe_ffn(x, route_idx, route_gate, layer, W['w1'], W['b1'], W['w2'], W['b2'], moe_rows)


PROMPT_CHUNKS = 2


def _trunk_prompt(x, W, lam, subln, lam_init):
    B, S, _ = x.shape
    bc = B // PROMPT_CHUNKS
    tc = bc * S
    cs = range(PROMPT_CHUNKS)
    x2d = x.reshape(B * S, D_MODEL)
    pj = [_inproj(x2d, W['w_in'], W['lb'], c * tc, tc) for c in cs]
    hg = [_hgrn_seq(*pj[c][:4], bc, S) for c in cs]
    m0 = [_mix0(hg[c][0], pj[c][4], x2d, W, c * tc) for c in cs]
    e0 = [_moe(*m0[c], 0, W, MOE_ROWS) for c in cs]
    f0 = [_ffn_out_seq(m0[c][0], e0[c][0], e0[c][1], W['ln_ffn_g'][0], W['ln_ffn_b'][0], W, bc, S) for c in cs]
    at = [_attn_seq(f0[c][5], f0[c][3], f0[c][4], lam, subln, lam_init, bc, S) for c in cs]
    m1 = [_mix1(at[c], f0[c][0], W) for c in cs]
    e1 = [_moe(*m1[c], 1, W, MOE_ROWS) for c in cs]
    x4 = [_ffn_out(m1[c][0], e1[c][0], e1[c][1], W['ln_ffn_g'][1], W['ln_ffn_b'][1], [])[0] for c in cs]
    y = jnp.concatenate(x4, axis=0).reshape(B, S, D_MODEL)
    state = jnp.concatenate([hg[c][1] for c in cs], axis=0)
    kt = jnp.concatenate([f0[c][1] for c in cs], axis=0)
    v3 = jnp.concatenate([f0[c][2] for c in cs], axis=0)
    k_out = jnp.transpose(kt.reshape(B, H_B, 2, HD_B, S), (0, 4, 1, 2, 3))
    return y, state[None], k_out, v3.reshape(B, S, H_B, 2 * HD_B)


def _trunk_sample(x, state0, cache_kt, cache_v, page_table, W, lam, subln, lam_init):
    B = x.shape[0]
    x2d = x.reshape(B, D_MODEL)
    qs, k, lf, v, gs = _inproj(x2d, W['w_in'], W['lb'], 0, B)
    o, state = _hgrn_step(state0, qs, k, lf, v)
    x1, r_idx, r_gate = _mix0(o, gs, x2d, W, 0)
    yk, gates = _moe(x1, r_idx, r_gate, 0, W, 32)
    x2, k_sh, v_sh, q = _ffn_out(x1, yk, gates, W['ln_ffn_g'][0], W['ln_ffn_b'][0],
                                 [W['w_k'], W['w_v'], W['w_q']])
    d = _attn_paged(q, k_sh, v_sh, cache_kt, cache_v, page_table, lam, subln, lam_init)
    x3, r_idx, r_gate = _mix1(d, x2, W)
    yk, gates = _moe(x3, r_idx, r_gate, 1, W, 32)
    (x4,) = _ffn_out(x3, yk, gates, W['ln_ffn_g'][1], W['ln_ffn_b'][1], [])
    return (x4.reshape(B, 1, D_MODEL), state[None],
            k_sh.reshape(B, 1, H_B, 2, HD_B), v_sh.reshape(B, 1, H_B, 2 * HD_B))


def kernel(x_prompt, x_sample, state_hgrn, cache_k, cache_v, page_table, w_in_a, lb_logits, gnorm_a, w_o_a,
           w_kv, w_q_b, lam_q1, lam_k1, lam_q2, lam_k2, subln_b, w_o_b, ln_mix_g, ln_mix_b, ln_ffn_g, ln_ffn_b,
           w_router, b_router, w1, b1, w2, b2):
    q_dim = H_B * 2 * HD_B
    lb = jnp.cumsum(jax.nn.softmax(lb_logits.astype(f32), axis=0), axis=0)[:N_A_LAYERS]
    W = dict(
        lb=lb[0].reshape(1, D_A), w_in=w_in_a[0].astype(bf16), gnorm=gnorm_a[0], w_o_a=w_o_a[0].astype(bf16),
        w_k=w_kv[:, :q_dim].astype(bf16), w_v=w_kv[:, q_dim:].astype(bf16), w_q=w_q_b[0].astype(bf16),
        w_q_scaled=(w_q_b[0] * (HD_B ** -0.5)).astype(bf16),
        w_o_b=w_o_b[0].astype(bf16), ln_mix_g=ln_mix_g, ln_mix_b=ln_mix_b, ln_ffn_g=ln_ffn_g, ln_ffn_b=ln_ffn_b,
        w_router=w_router, b_router=b_router, w1=w1, b1=b1, w2=w2, b2=b2)
    layer = N_A_LAYERS
    lam_init = 0.8 - 0.6 * math.exp(-0.3 * layer)
    lam = (jnp.exp(jnp.sum((lam_q1[0] * lam_k1[0]).astype(f32)))
           - jnp.exp(jnp.sum((lam_q2[0] * lam_k2[0]).astype(f32))) + lam_init)
    subln = subln_b[0]

    y_p, st_p, k_p, v_p = _trunk_prompt(x_prompt, W, lam, subln, lam_init)

    n_pool = cache_k.shape[0]
    cache_kt = jnp.transpose(cache_k, (0, 2, 3, 4, 1)).reshape(n_pool, D_MODEL, PAGE_SIZE)
    y_s, st_s, k_s, v_s = _trunk_sample(x_sample, state_hgrn[0], cache_kt, cache_v, page_table, W,
                                        lam, subln, lam_init)
    return (y_p, y_s, st_p, st_s, k_p, v_p, k_s, v_s)
```

```python
import functools
import math

import numpy as np
import jax
import jax.numpy as jnp
from jax import lax
from jax.experimental import pallas as pl
from jax.experimental.pallas import tpu as pltpu

f32 = jnp.float32
bf16 = jnp.bfloat16

D_MODEL = 1024
DEPTH = 2
N_A_LAYERS = 1
H_A = 8
DK_A = 128
DV_A = 128
D_A = H_A * DK_A
HD_B = 64
H_B = 8
N_EXPERTS = 32
TOP_K = 4
D_FF = D_MODEL
SWIGLU_LIMIT = 7.0
SWIGLU_ALPHA = 1.702
DEEPNORM_ALPHA = (2.0 * DEPTH) ** 0.25
LN_EPS = 1e-5
NEG_INF = -1e30
PAGE_SIZE = 128
LOG2_E = 1.4426950408889634

LANES = 128
ROW_SLAB = D_MODEL // LANES
VMEM_LIMIT = 56 * 1024 * 1024
MOE_VMEM_LIMIT = 60 * 1024 * 1024

HGRN_TILE = 128
HGRN_HEADS_PER_STEP = 2
MOE_ROWS = 512


def _cparams(sem):
    return pltpu.CompilerParams(dimension_semantics=sem, vmem_limit_bytes=VMEM_LIMIT)


def _sigmoid(x):
    return 1.0 / (1.0 + jnp.exp(-x))


def _layer_norm_rows(x, g, b):
    mu = jnp.mean(x, -1, keepdims=True)
    xc = x - mu
    var = jnp.mean(xc * xc, -1, keepdims=True)
    return xc * lax.rsqrt(var + LN_EPS) * g + b


def _dot(a, b):
    return jnp.dot(a, b, preferred_element_type=f32)


def _dot_nt(a, b):
    return lax.dot_general(a, b, (((1,), (1,)), ((), ())), preferred_element_type=f32)


def _dot_tn(a, b):
    return lax.dot_general(a, b, (((0,), (0,)), ((), ())), preferred_element_type=f32)


def _inproj_kernel(x_ref, w_ref, lb_ref, qs_ref, k_ref, lf_ref, v_ref, gs_ref):
    h = _dot(x_ref[...].astype(bf16), w_ref[...])
    q = h[:, :D_A]
    fr = h[:, D_A:2 * D_A]
    lb = lb_ref[...]
    z = jnp.exp(-jnp.abs(fr))
    r = 1.0 / (1.0 + z)
    zr = z * r
    pos = fr >= 0
    sig = jnp.where(pos, r, zr)
    nsig = jnp.where(pos, zr, r)
    qs_ref[...] = q * _sigmoid(q)
    k_ref[...] = (1.0 - lb) * nsig
    lf_ref[...] = jnp.log(lb + (1.0 - lb) * sig)
    v_ref[...] = h[:, 2 * D_A:3 * D_A]
    g = h[:, 3 * D_A:]
    gs_ref[...] = g * _sigmoid(g)


def _inproj(x2d, w_in, lb, row0, T):
    tm = min(256, T)
    blk0 = row0 // tm
    out = jax.ShapeDtypeStruct((T, D_A), f32)
    row = pl.BlockSpec((tm, D_A), lambda i: (i, 0))
    return pl.pallas_call(
        _inproj_kernel,
        grid=(T // tm,),
        in_specs=[pl.BlockSpec((tm, D_MODEL), lambda i: (i + blk0, 0)),
                  pl.BlockSpec((D_MODEL, 4 * D_A), lambda i: (0, 0)),
                  pl.BlockSpec((1, D_A), lambda i: (0, 0))],
        out_specs=[row] * 5,
        out_shape=[out] * 5,
        compiler_params=_cparams(("parallel",)),
        name="hgrn_inproj",
    )(x2d, w_in, lb)


def _hgrn_levels(R):
    n = R // 2
    out = []
    while n >= 1:
        out.append(n)
        n //= 2
    return out


def _hgrn_masks(R):
    t = np.arange(R)[:, None]
    s = np.arange(R)[None, :]
    ms = []
    for n in _hgrn_levels(R):
        ms.append((t // (2 * n) == s // (2 * n)) & (t % (2 * n) >= n) & (s % (2 * n) < n))
    ms.append(t == s)
    return np.stack(ms).astype(np.float32)


def _level_ref_rows(b, n):
    R = b.shape[0]
    if 2 * n >= 8:
        b3 = b.reshape(R // (2 * n), 2 * n, LANES)
        return jnp.broadcast_to(b3[:, n - 1:n, :], b3.shape).reshape(R, LANES)
    b3 = b.reshape(R // 8, 8, LANES)
    sub = lax.broadcasted_iota(jnp.int32, b3.shape, 1)
    out = None
    for r0 in range(0, 8, 2 * n):
        row = jnp.broadcast_to(b3[:, r0 + n - 1:r0 + n, :], b3.shape)
        out = row if out is None else jnp.where(sub >= r0, row, out)
    return out.reshape(R, LANES)


def _split3(x):
    hi = x.astype(bf16)
    r1 = x - hi.astype(f32)
    mid = r1.astype(bf16)
    lo = (r1 - mid.astype(f32)).astype(bf16)
    return hi, mid, lo


def _hgrn_seq_kernel(qs_ref, k_ref, lf_ref, v_ref, masks_ref, tril_ref, o_ref, st_ref, state_t):
    R = HGRN_TILE
    levels = _hgrn_levels(R)
    step = pl.program_id(2)

    @pl.when(step == 0)
    def _():
        state_t[...] = jnp.zeros_like(state_t)

    def tile_head(r, hh):
        cols = slice(hh * LANES, (hh + 1) * LANES)
        q = qs_ref[pl.ds(r, R), cols]
        k = k_ref[pl.ds(r, R), cols]
        v = v_ref[pl.ds(r, R), cols].astype(bf16)
        hi, mid, lo = _split3(lf_ref[pl.ds(r, R), cols] * LOG2_E)
        c3 = _dot(tril_ref[...], jnp.concatenate([hi, mid, lo], axis=1))
        b = c3[:, :LANES] + c3[:, LANES:2 * LANES] + c3[:, 2 * LANES:]
        a = masks_ref[len(levels)] * _dot_nt(q.astype(bf16), k.astype(bf16))
        for li, n in enumerate(levels):
            e = jnp.exp2(-jnp.abs(b - _level_ref_rows(b, n)))
            a = a + masks_ref[li] * _dot_nt((q * e).astype(bf16), (k * e).astype(bf16))
        st = state_t[hh]
        b_last = b[R - 1:R, :]
        qd = (q * jnp.exp2(b)).astype(bf16)
        kd = (k * jnp.exp2(b_last - b)).astype(bf16)
        o_ref[pl.ds(r, R), cols] = _dot(a.astype(bf16), v) + _dot_nt(qd, st.astype(bf16))
        state_t[hh] = st * jnp.exp2(b_last) + _dot_tn(v, kd)

    def tile(i, carry):
        r = pl.multiple_of(i * R, R)
        for hh in range(HGRN_HEADS_PER_STEP):
            tile_head(r, hh)
        return carry

    lax.fori_loop(0, qs_ref.shape[0] // R, tile, 0)

    @pl.when(step == pl.num_programs(2) - 1)
    def _():
        st_ref[0] = state_t[...]


def _hgrn_seq(qs, k, lf, v, B, S):
    R = HGRN_TILE
    G = HGRN_HEADS_PER_STEP
    rb = min(512, S)
    nst = S // rb
    masks = jnp.asarray(_hgrn_masks(R))
    tril = jnp.asarray(np.tril(np.ones((R, R), np.float32)), dtype=bf16)
    row = pl.BlockSpec((rb, G * LANES), lambda b, h, j: (b * nst + j, h))
    o, st_t = pl.pallas_call(
        _hgrn_seq_kernel,
        grid=(B, H_A // G, nst),
        in_specs=[row, row, row, row,
                  pl.BlockSpec(masks.shape, lambda b, h, j: (0, 0, 0)),
                  pl.BlockSpec((R, R), lambda b, h, j: (0, 0))],
        out_specs=[row, pl.BlockSpec((1, G, DV_A, DK_A), lambda b, h, j: (b, h, 0, 0))],
        out_shape=[jax.ShapeDtypeStruct((B * S, D_A), f32),
                   jax.ShapeDtypeStruct((B, H_A, DV_A, DK_A), f32)],
        scratch_shapes=[pltpu.VMEM((G, DV_A, DK_A), f32)],
        compiler_params=_cparams(("parallel", "parallel", "arbitrary")),
        name="hgrn_seq",
    )(qs, k, lf, v, masks, tril)
    return o, jnp.swapaxes(st_t, 2, 3)


def _hgrn_step_kernel(s_ref, q_ref, k_ref, lf_ref, v_ref, o_ref, so_ref):
    for h in range(H_A):
        s_new = jnp.exp(lf_ref[0, h]) * s_ref[0, h] + k_ref[0, h] * v_ref[0, h]
        so_ref[0, h] = s_new
        o_ref[0, h] = jnp.sum(q_ref[0, h] * s_new, axis=0, keepdims=True)


def _hgrn_step(state, qs, k, lf, v):
    B = state.shape[0]
    col = lambda t: t.reshape(B, H_A, DK_A, 1)
    cspec = pl.BlockSpec((1, H_A, DK_A, 1), lambda b: (b, 0, 0, 0))
    rspec = pl.BlockSpec((1, H_A, 1, DV_A), lambda b: (b, 0, 0, 0))
    sspec = pl.BlockSpec((1, H_A, DK_A, DV_A), lambda b: (b, 0, 0, 0))
    o, s_new = pl.pallas_call(
        _hgrn_step_kernel,
        grid=(B,),
        in_specs=[sspec, cspec, cspec, cspec, rspec],
        out_specs=[rspec, sspec],
        out_shape=[jax.ShapeDtypeStruct((B, H_A, 1, DV_A), f32),
                   jax.ShapeDtypeStruct(state.shape, f32)],
        compiler_params=_cparams(("parallel",)),
        name="hgrn_step",
    )(state, col(qs), col(k), col(lf), v.reshape(B, H_A, 1, DV_A))
    return o.reshape(B, D_A), s_new


def _mix_out_kernel(gated, *refs):
    if gated:
        m_ref, gs_ref, gn_ref, x_ref, wo_ref, lg_ref, lbias_ref, wr_ref, br_ref, x1_ref, xg_ref, idx_ref, gate_ref = refs
        m = m_ref[...] * gs_ref[...]
        m = m * lax.rsqrt(jnp.mean(m * m, -1, keepdims=True) + LN_EPS) * gn_ref[...]
        m = m.astype(bf16)
    else:
        m_ref, x_ref, wo_ref, lg_ref, lbias_ref, wr_ref, br_ref, x1_ref, xg_ref, idx_ref, gate_ref = refs
        m = m_ref[...]
    mix = _dot(m, wo_ref[...])
    x1 = _layer_norm_rows(DEEPNORM_ALPHA * x_ref[...] + mix, lg_ref[...], lbias_ref[...])
    x1_ref[...] = x1
    for j in range(ROW_SLAB):
        xg_ref[:, j, :] = x1[:, j * LANES:(j + 1) * LANES]
    logits =jnp.dot(x1, wr_ref[...], preferred_element_type=f32,
                     precision=lax.Precision.HIGHEST) + br_ref[...]
    lane = lax.broadcasted_iota(jnp.int32, logits.shape, 1).astype(f32)
    cur = logits
    idx_out = jnp.zeros_like(logits)
    gate_out = jnp.zeros_like(logits)
    denom = None
    top = None
    for kk in range(TOP_K):
        v = jnp.max(cur, -1, keepdims=True)
        idx = jnp.min(jnp.where(cur == v, lane, float(N_EXPERTS)), -1, keepdims=True)
        top = v if top is None else top
        e = jnp.exp(v - top)
        denom = e if denom is None else denom + e
        idx_out = jnp.where(lane == kk, idx, idx_out)
        gate_out = jnp.where(lane == kk, e, gate_out)
        cur = jnp.where(lane == idx, -jnp.inf, cur)
    idx_ref[...] = idx_out.astype(jnp.int32)
    gate_ref[...] = gate_out / denom


def _mix_out(m, gate, x2d, w_o, ln_g, ln_b, w_router, b_router, row0=0):
    T = m.shape[0]
    tm = min(256, T)
    blk0 = row0 // tm
    row = pl.BlockSpec((tm, D_MODEL), lambda i: (i, 0))
    vec = pl.BlockSpec((1, D_MODEL), lambda i: (0, 0))
    gated = gate is not None
    ins = [m] + ([gate[0], gate[1].reshape(1, -1)] if gated else []) + [
        x2d, w_o, ln_g.reshape(1, -1), ln_b.reshape(1, -1), w_router, b_router.reshape(1, -1)]
    specs = [row] + ([row, vec] if gated else []) + [
        pl.BlockSpec((tm, D_MODEL), lambda i: (i + blk0, 0)),
        pl.BlockSpec((D_MODEL, D_MODEL), lambda i: (0, 0)), vec, vec,
        pl.BlockSpec((D_MODEL, N_EXPERTS), lambda i: (0, 0)),
        pl.BlockSpec((1, N_EXPERTS), lambda i: (0, 0))]
    return pl.pallas_call(
        functools.partial(_mix_out_kernel, gated),
        grid=(T // tm,),
        in_specs=specs,
        out_specs=[row, pl.BlockSpec((tm, ROW_SLAB, LANES), lambda i: (i, 0, 0))]
                  + [pl.BlockSpec((tm, N_EXPERTS), lambda i: (i, 0))] * 2,
        out_shape=[jax.ShapeDtypeStruct((T, D_MODEL), f32),
                   jax.ShapeDtypeStruct((T, ROW_SLAB, LANES), f32),
                   jax.ShapeDtypeStruct((T, N_EXPERTS), jnp.int32),
                   jax.ShapeDtypeStruct((T, N_EXPERTS), f32)],
        compiler_params=_cparams(("parallel",)),
        name="mix_out_gated" if gated else "mix_out",
    )(*ins)


MOE_FF_CHUNK = 512


def _moe_kernel(blk_e_ref, n_used_ref, tok_ref, x_hbm, w1_ref, b1_ref, w2_ref, b2_ref, y_ref,
                w1b, w2b, xbuf, sem):
    i = pl.program_id(0)
    bm = y_ref.shape[0]
    n_used = n_used_ref[0]
    live = i < n_used
    slot = lax.rem(i, 2)

    def row_copy(src_row, dst_row, s):
        return pltpu.make_async_copy(x_hbm.at[pl.ds(src_row, ROW_SLAB)],
                                     xbuf.at[s, pl.ds(dst_row, ROW_SLAB)], sem.at[s])

    def start_block(blk, s):
        def body(r, carry):
            tok = tok_ref[blk * bm + r]
            row_copy(pl.multiple_of(tok * ROW_SLAB, ROW_SLAB), pl.multiple_of(r * ROW_SLAB, ROW_SLAB), s).start()
            return carry
        lax.fori_loop(0, bm, body, 0, unroll=8)

    @pl.when(live & (i == 0))
    def _():
        start_block(0, 0)

    @pl.when(i + 1 < n_used)
    def _():
        start_block(i + 1, 1 - slot)

    prev = blk_e_ref[jnp.maximum(i - 1, 0)]

    @pl.when(live & ((i == 0) | (blk_e_ref[i] != prev)))
    def _():
        w1b[...] = w1_ref[0, 0].astype(bf16)
        w2b[...] = w2_ref[0, 0].astype(bf16)

    @pl.when(live)
    def _():
        pltpu.make_async_copy(x_hbm.at[pl.ds(0, bm * ROW_SLAB)], xbuf.at[slot], sem.at[slot]).wait()
        x = jnp.concatenate([xbuf[slot, pl.ds(j, bm, stride=ROW_SLAB), :] for j in range(ROW_SLAB)],
                            axis=1).astype(bf16)
        y = None
        for c in range(0, D_FF, MOE_FF_CHUNK):
            hg = _dot(x, w1b[:, c:c + MOE_FF_CHUNK]) + b1_ref[0, 0, :, c:c + MOE_FF_CHUNK]
            hl = (_dot(x, w1b[:, D_FF + c:D_FF + c + MOE_FF_CHUNK])
                  + b1_ref[0, 0, :, D_FF + c:D_FF + c + MOE_FF_CHUNK])
            x_glu = jnp.minimum(hg, SWIGLU_LIMIT)
            x_lin = jnp.clip(hl, -SWIGLU_LIMIT, SWIGLU_LIMIT)
            act = x_glu * _sigmoid(SWIGLU_ALPHA * x_glu) * (x_lin + 1.0)
            part = _dot(act.astype(bf16), w2b[c:c + MOE_FF_CHUNK, :])
            y = part if y is None else y + part
        y_ref[...] = (y + b2_ref[0, 0]).astype(y_ref.dtype)

    @pl.when(jnp.logical_not(live))
    def _():
        y_ref[...] = jnp.zeros_like(y_ref)


def _moe_rows(x_slabs, row_tok, blk_e, n_used, layer, w1, b1, w2, b2, bm):
    nb = row_tok.shape[0] // bm
    grid_spec = pltpu.PrefetchScalarGridSpec(
        num_scalar_prefetch=3,
        grid=(nb,),
        in_specs=[pl.BlockSpec(memory_space=pl.ANY),
                  pl.BlockSpec((1, 1, D_MODEL, 2 * D_FF), lambda i, e, n, t: (layer, e[i], 0, 0)),
                  pl.BlockSpec((1, 1, 1, 2 * D_FF), lambda i, e, n, t: (layer, e[i], 0, 0)),
                  pl.BlockSpec((1, 1, D_FF, D_MODEL), lambda i, e, n, t: (layer, e[i], 0, 0)),
                  pl.BlockSpec((1, 1, 1, D_MODEL), lambda i, e, n, t: (layer, e[i], 0, 0))],
        out_specs=pl.BlockSpec((bm, D_MODEL), lambda i, e, n, t: (i, 0)),
        scratch_shapes=[pltpu.VMEM((D_MODEL, 2 * D_FF), bf16), pltpu.VMEM((D_FF, D_MODEL), bf16),
                        pltpu.VMEM((2, bm * ROW_SLAB, LANES), f32), pltpu.SemaphoreType.DMA((2,))],
    )
    return pl.pallas_call(
        _moe_kernel,
        grid_spec=grid_spec,
        out_shape=jax.ShapeDtypeStruct((nb * bm, D_MODEL), bf16),
        compiler_params=pltpu.CompilerParams(dimension_semantics=("arbitrary",),
                                             vmem_limit_bytes=MOE_VMEM_LIMIT),
        name="moe_experts",
    )(blk_e, n_used, row_tok, x_slabs, w1, b1.reshape(DEPTH, N_EXPERTS, 1, -1), w2,
      b2.reshape(DEPTH, N_EXPERTS, 1, -1))


def _moe_ffn(x_slabs, route_idx, route_gate, layer, w1, b1, w2, b2, bm):
    T = x_slabs.shape[0]
    n_slots = T * TOP_K
    top_idx = route_idx[:, :TOP_K]
    gates = route_gate[:, :TOP_K]
    flat_e = top_idx.reshape(-1)
    order = jnp.argsort(flat_e)
    inv = jnp.argsort(order)
    counts = jnp.bincount(flat_e, length=N_EXPERTS)
    padded = (counts + bm - 1) // bm * bm
    pad_end = jnp.cumsum(padded)
    pad_start = pad_end - padded
    start = jnp.cumsum(counts) - counts
    n_blocks = (n_slots + N_EXPERTS * (bm - 1) + bm - 1) // bm
    blk_e = jnp.minimum(jnp.sum(pad_end[None, :] <= (jnp.arange(n_blocks) * bm)[:, None], axis=1),
                        N_EXPERTS - 1).astype(jnp.int32)
    n_used = (pad_end[-1] // bm).astype(jnp.int32).reshape(1)
    row = jnp.arange(n_blocks * bm)
    row_e = jnp.repeat(blk_e, bm)
    j = row - pad_start[row_e]
    src = jnp.clip(start[row_e] + j, 0, n_slots - 1)
    row_tok = jnp.where((j < counts[row_e]) & (row < pad_end[-1]), order[src] // TOP_K, 0).astype(jnp.int32)
    y_rows = _moe_rows(x_slabs.reshape(T * ROW_SLAB, LANES), row_tok, blk_e, n_used, layer, w1, b1, w2, b2, bm)
    pos = (pad_start[flat_e] + inv - start[flat_e]).astype(jnp.int32)
    pos_k = pos.reshape(T, TOP_K).T.reshape(-1)
    return y_rows[pos_k].reshape(TOP_K, T, D_MODEL), gates


def _ffn_combine(x_ref, y_refs, gt_ref, g_ref, b_ref):
    gt = gt_ref[...]
    y = y_refs[0][0].astype(f32) * gt[:, 0:1]
    for kk in range(1, TOP_K):
        y = y + y_refs[kk][0].astype(f32) * gt[:, kk:kk + 1]
    return _layer_norm_rows(DEEPNORM_ALPHA * x_ref[...] + y, g_ref[...], b_ref[...])


def _ffn_out_kernel(n_proj, x_ref, y0, y1, y2, y3, gt_ref, g_ref, b_ref, *rest):
    x2 = _ffn_combine(x_ref, (y0, y1, y2, y3), gt_ref, g_ref, b_ref)
    rest[n_proj][...] = x2
    xb = x2.astype(bf16)
    for p in range(n_proj):
        rest[n_proj + 1 + p][...] = _dot(xb, rest[p][...])


def _ffn_in_specs(tm, row_index):
    row = pl.BlockSpec((tm, D_MODEL), lambda *g: (row_index(*g), 0))
    yk = [pl.BlockSpec((1, tm, D_MODEL), functools.partial(lambda kk, *g: (kk, row_index(*g), 0), kk))
          for kk in range(TOP_K)]
    vec = pl.BlockSpec((1, D_MODEL), lambda *g: (0, 0))
    return row, [row] + yk + [pl.BlockSpec((tm, TOP_K), lambda *g: (row_index(*g), 0)), vec, vec]


def _ffn_out(x1, yk, gates, ln_g, ln_b, proj_ws):
    T = x1.shape[0]
    tm = min(256, T)
    row, specs = _ffn_in_specs(tm, lambda i: i)
    n_proj = len(proj_ws)
    return pl.pallas_call(
        functools.partial(_ffn_out_kernel, n_proj),
        grid=(T // tm,),
        in_specs=specs + [pl.BlockSpec(w.shape, lambda i: (0, 0)) for w in proj_ws],
        out_specs=[row] + [pl.BlockSpec((tm, w.shape[1]), lambda i: (i, 0)) for w in proj_ws],
        out_shape=[jax.ShapeDtypeStruct((T, D_MODEL), f32)] +
                  [jax.ShapeDtypeStruct((T, w.shape[1]), f32) for w in proj_ws],
        compiler_params=_cparams(("parallel",)),
        name="ffn_out_%d" % n_proj,
    )(x1, yk, yk, yk, yk, gates, ln_g.reshape(1, -1), ln_b.reshape(1, -1), *proj_ws)


def _ffn_out_seq_kernel(x_ref, y0, y1, y2, y3, gt_ref, g_ref, b_ref, wk_ref, wkt_ref, wv_ref, wvt_ref, wqt_ref,
                        x2_ref, kt_ref, v3_ref, kb_ref, vt_ref, qt_ref):
    x2 = _ffn_combine(x_ref, (y0, y1, y2, y3), gt_ref, g_ref, b_ref)
    x2_ref[...] = x2
    xb = x2.astype(bf16)
    kb_ref[...] = _dot(xb, wk_ref[...]).astype(bf16)
    kt_ref[0] = _dot_nt(wkt_ref[...], xb)
    vt_ref[0] = _dot_nt(wvt_ref[...], xb).astype(bf16)
    qt_ref[0] = _dot_nt(wqt_ref[...], xb).astype(bf16)
    v = _dot(xb, wv_ref[...])
    for h in range(H_B):
        v3_ref[:, h, :] = v[:, h * LANES:(h + 1) * LANES]


def _ffn_out_seq(x1, yk, gates, ln_g, ln_b, W, B, S):
    T = B * S
    tm = 256
    nj = S // tm
    row, specs = _ffn_in_specs(tm, lambda b, j: b * nj + j)
    wspec = pl.BlockSpec((D_MODEL, D_MODEL), lambda b, j: (0, 0), pipeline_mode=pl.Buffered(1))
    tspec = pl.BlockSpec((1, D_MODEL, tm), lambda b, j: (b, 0, j))
    return pl.pallas_call(
        _ffn_out_seq_kernel,
        grid=(B, nj),
        in_specs=specs + [wspec] * 5,
        out_specs=[row, tspec, pl.BlockSpec((tm, H_B, LANES), lambda b, j: (b * nj + j, 0, 0)), row, tspec, tspec],
        out_shape=[jax.ShapeDtypeStruct((T, D_MODEL), f32),
                   jax.ShapeDtypeStruct((B, D_MODEL, S), f32),
                   jax.ShapeDtypeStruct((T, H_B, LANES), f32),
                   jax.ShapeDtypeStruct((T, D_MODEL), bf16),
                   jax.ShapeDtypeStruct((B, D_MODEL, S), bf16),
                   jax.ShapeDtypeStruct((B, D_MODEL, S), bf16)],
        compiler_params=_cparams(("parallel", "parallel")),
        name="ffn_out_seq",
    )(x1, yk, yk, yk, yk, gates, ln_g.reshape(1, -1), ln_b.reshape(1, -1),
      W['w_k'], W['w_k'].T, W['w_v'], W['w_v'].T, W['w_q_scaled'].T)


ATT_T = 512


def _attn_seq_kernel(lam_init, slopes_ref, lam_ref, qt_ref, k_ref, vt_ref, g_ref, o_ref,
                     qt2_scr, bias_scr, bias_diag_scr, acc_scr):
    t = ATT_T
    h = pl.program_id(1)
    i = pl.program_id(2)
    slope = slopes_ref[h]

    @pl.when(i == 0)
    def _():
        kk = lax.broadcasted_iota(jnp.int32, (t, 2 * t), 0)
        qq = lax.broadcasted_iota(jnp.int32, (t, 2 * t), 1)
        qq = jnp.where(qq >= t, qq - t, qq)
        rel = slope * (kk - qq).astype(f32)
        bias_scr[...] = rel
        bias_diag_scr[...] = jnp.where(kk <= qq, rel, NEG_INF)

    qt = qt_ref[0]
    dim = lax.broadcasted_iota(jnp.int32, qt.shape, 0)
    zero = jnp.zeros_like(qt)
    qt2_scr[:, :t] = jnp.where(dim < HD_B, qt, zero)
    qt2_scr[:, t:] = jnp.where(dim >= HD_B, qt, zero)
    acc_scr[...] = jnp.zeros_like(acc_scr)

    def kv_block(j, bias_ref, m_old, l_old):
        c = pl.multiple_of(j * t, t)
        s = _dot(k_ref[pl.ds(c, t), :], qt2_scr[...]) + bias_ref[...]
        shift = slope * ((i - j) * t).astype(f32)
        m_new = jnp.maximum(m_old, jnp.max(s, 0, keepdims=True) - shift)
        p = jnp.exp(s - (m_new + shift))
        alpha = jnp.exp(m_old - m_new)
        l_new = alpha * l_old + jnp.sum(p, 0, keepdims=True)
        acc_scr[...] = alpha * acc_scr[...] + _dot(vt_ref[0, :, pl.ds(c, t)], p.astype(bf16))
        return m_new, l_new

    m0 = jnp.full((1, 2 * t), NEG_INF, f32)
    l0 = jnp.zeros((1, 2 * t), f32)
    m, l = lax.fori_loop(0, i, lambda j, ml: kv_block(j, bias_scr, *ml), (m0, l0))
    m, l = kv_block(i, bias_diag_scr, m, l)

    o = acc_scr[...] / l
    d = o[:, :t] - lam_ref[0] * o[:, t:]
    d = d * lax.rsqrt(jnp.mean(d * d, 0, keepdims=True) + LN_EPS) * g_ref[...]
    o_ref[...] = (d * (1.0 - lam_init)).T.astype(o_ref.dtype)


def _attn_seq(qt, kb, vt, lam, subln, lam_init, B, S):
    t = ATT_T
    nq = S // t
    slopes = jnp.exp2(-8.0 * jnp.arange(1, H_B + 1, dtype=f32) / H_B)
    smem = pl.BlockSpec(memory_space=pltpu.SMEM)
    return pl.pallas_call(
        functools.partial(_attn_seq_kernel, lam_init),
        grid=(B, H_B, nq),
        in_specs=[smem, smem,
                  pl.BlockSpec((1, LANES, t), lambda b, h, i: (b, h, i)),
                  pl.BlockSpec((S, LANES), lambda b, h, i: (b, h)),
                  pl.BlockSpec((1, LANES, S), lambda b, h, i: (b, h, 0)),
                  pl.BlockSpec((LANES, 1), lambda b, h, i: (0, 0))],
        out_specs=pl.BlockSpec((t, LANES), lambda b, h, i: (b * nq + i, h)),
        out_shape=jax.ShapeDtypeStruct((B * S, D_MODEL), bf16),
        scratch_shapes=[pltpu.VMEM((LANES, 2 * t), bf16),
                        pltpu.VMEM((t, 2 * t), f32),
                        pltpu.VMEM((t, 2 * t), f32),
                        pltpu.VMEM((LANES, 2 * t), f32)],
        compiler_params=_cparams(("parallel", "arbitrary", "arbitrary")),
        name="attn_seq",
    )(slopes, lam.reshape(1), qt, kb, vt, subln.reshape(-1, 1))


PAGES_PER_STEP = 8


def _attn_paged_kernel(lam_init, n_pages, pt_ref, slopes_ref, lam_ref, q_ref, kn_ref, vn_ref, g_ref, *rest):
    G = PAGES_PER_STEP
    k_refs = rest[:G]
    v_refs = rest[G:2 * G]
    o_ref, qbd_scr, m_scr, l_scr, acc_scr = rest[2 * G:]
    step = pl.program_id(1)
    nrow = 2 * H_B
    r1 = lax.broadcasted_iota(jnp.int32, (nrow, 1), 0)
    slope_col = jnp.zeros((nrow, 1), f32)
    for hh in range(H_B):
        slope_col = jnp.where(r1 // 2 == hh, slopes_ref[hh], slope_col)

    @pl.when(step == 0)
    def _():
        rowi = lax.broadcasted_iota(jnp.int32, (nrow, D_MODEL), 0)
        coli = lax.broadcasted_iota(jnp.int32, (nrow, D_MODEL), 1)
        own = (coli // HD_B) == rowi
        q = q_ref[0] * (HD_B ** -0.5)
        qbd = jnp.where(own, jnp.broadcast_to(q, (nrow, D_MODEL)), 0.0).astype(bf16)
        qbd_scr[...] = qbd
        m_scr[...] = jnp.sum(qbd.astype(f32) * kn_ref[0].astype(bf16).astype(f32), -1, keepdims=True)
        l_scr[...] = jnp.ones_like(l_scr)
        acc_scr[...] = jnp.broadcast_to(vn_ref[0].astype(bf16).astype(f32), (nrow, D_MODEL))

    s = jnp.concatenate([_dot(qbd_scr[...], k_refs[u][0].astype(bf16)) for u in range(G)], axis=1)
    tok = lax.broadcasted_iota(jnp.int32, s.shape, 1)
    dist = ((n_pages - step * G) * PAGE_SIZE - tok).astype(f32)
    s = s - slope_col * dist
    m = m_scr[...]
    m_new = jnp.maximum(m, jnp.max(s, -1, keepdims=True))
    p32 = jnp.exp(s - m_new)
    alpha = jnp.exp(m - m_new)
    l = alpha * l_scr[...] + jnp.sum(p32, -1, keepdims=True)
    p = p32.astype(bf16)
    acc = alpha * acc_scr[...]
    for u in range(G):
        vb = jnp.concatenate([v_refs[u][0, pl.ds(hh, PAGE_SIZE, stride=H_B), :] for hh in range(H_B)],
                             axis=1).astype(bf16)
        acc = acc + _dot(p[:, u * PAGE_SIZE:(u + 1) * PAGE_SIZE], vb)
    m_scr[...] = m_new
    l_scr[...] = l
    acc_scr[...] = acc

    @pl.when(step == pl.num_programs(1) - 1)
    def _():
        o = acc / l
        for hh in range(H_B):
            blk = o[2 * hh:2 * hh + 2, hh * LANES:(hh + 1) * LANES]
            d = blk[0:1, :] - lam_ref[0] * blk[1:2, :]
            d = d * lax.rsqrt(jnp.mean(d * d, -1, keepdims=True) + LN_EPS) * g_ref[...]
            o_ref[0, :, hh * LANES:(hh + 1) * LANES] = (d * (1.0 - lam_init)).astype(o_ref.dtype)


def _attn_paged(q, k_new, v_new, cache_kt, cache_v, page_table, lam, subln, lam_init):
    B = q.shape[0]
    G = PAGES_PER_STEP
    n_pages = page_table.shape[1]
    slopes = jnp.exp2(-8.0 * jnp.arange(1, H_B + 1, dtype=f32) / H_B)
    smem = pl.BlockSpec(memory_space=pltpu.SMEM)
    vec = pl.BlockSpec((1, 1, D_MODEL), lambda b, s, pt: (b, 0, 0))

    def page_index(u, ndim):
        return lambda b, s, pt: (pt[b * n_pages + s * G + u],) + (0,) * (ndim - 1)

    k_specs = [pl.BlockSpec((1, D_MODEL, PAGE_SIZE), page_index(u, 3)) for u in range(G)]
    v_specs = [pl.BlockSpec((1, PAGE_SIZE * H_B, LANES), page_index(u, 3)) for u in range(G)]
    grid_spec = pltpu.PrefetchScalarGridSpec(
        num_scalar_prefetch=1,
        grid=(B, n_pages // G),
        in_specs=[smem, smem, vec, vec, vec, pl.BlockSpec((1, LANES), lambda b, s, pt: (0, 0))] + k_specs + v_specs,
        out_specs=vec,
        scratch_shapes=[pltpu.VMEM((2 * H_B, D_MODEL), bf16),
                        pltpu.VMEM((2 * H_B, 1), f32),
                        pltpu.VMEM((2 * H_B, 1), f32),
                        pltpu.VMEM((2 * H_B, D_MODEL), f32)],
    )
    d = pl.pallas_call(
        functools.partial(_attn_paged_kernel, lam_init, n_pages),
        grid_spec=grid_spec,
        out_shape=jax.ShapeDtypeStruct((B, 1, D_MODEL), bf16),
        compiler_params=_cparams(("parallel", "arbitrary")),
        name="attn_paged",
    )(page_table.reshape(-1), slopes, lam.reshape(1), q.reshape(B, 1, -1), k_new.reshape(B, 1, -1),
      v_new.reshape(B, 1, -1), subln.reshape(1, -1), *([cache_kt] * G),
      *([cache_v.reshape(-1, PAGE_SIZE * H_B, LANES)] * G))
    return d.reshape(B, D_MODEL)


def _mix0(o, gs, x2d, W, row0):
    return _mix_out(o, (gs, W['gnorm']), x2d, W['w_o_a'], W['ln_mix_g'][0], W['ln_mix_b'][0],
                    W['w_router'][0], W['b_router'][0], row0)


def _mix1(d, x2, W):
    return _mix_out(d, None, x2, W['w_o_b'], W['ln_mix_g'][1], W['ln_mix_b'][1],
                    W['w_router'][1], W['b_router'][1])


def _moe(x, route_idx, route_gate, layer, W, moe_rows):
    return _moe_ffn(x, route_idx, route_gate, layer, W['w1'], W['b1'], W['w2'], W['b2'], moe_rows)


PROMPT_CHUNKS = 2


def _trunk_prompt(x, W, lam, subln, lam_init):
    B, S, _ = x.shape
    bc = B // PROMPT_CHUNKS
    tc = bc * S
    cs = range(PROMPT_CHUNKS)
    x2d = x.reshape(B * S, D_MODEL)
    pj = [_inproj(x2d, W['w_in'], W['lb'], c * tc, tc) for c in cs]
    hg = [_hgrn_seq(*pj[c][:4], bc, S) for c in cs]
    m0 = [_mix0(hg[c][0], pj[c][4], x2d, W, c * tc) for c in cs]
    e0 = [_moe(*m0[c][1:], 0, W, MOE_ROWS) for c in cs]
    f0 = [_ffn_out_seq(m0[c][0], e0[c][0], e0[c][1], W['ln_ffn_g'][0], W['ln_ffn_b'][0], W, bc, S) for c in cs]
    at = [_attn_seq(f0[c][5], f0[c][3], f0[c][4], lam, subln, lam_init, bc, S) for c in cs]
    m1 = [_mix1(at[c], f0[c][0], W) for c in cs]
    e1 = [_moe(*m1[c][1:], 1, W, MOE_ROWS) for c in cs]
    x4 = [_ffn_out(m1[c][0], e1[c][0], e1[c][1], W['ln_ffn_g'][1], W['ln_ffn_b'][1], [])[0] for c in cs]
    y = jnp.concatenate(x4, axis=0).reshape(B, S, D_MODEL)
    state = jnp.concatenate([hg[c][1] for c in cs], axis=0)
    kt = jnp.concatenate([f0[c][1] for c in cs], axis=0)
    v3 = jnp.concatenate([f0[c][2] for c in cs], axis=0)
    k_out = jnp.transpose(kt.reshape(B, H_B, 2, HD_B, S), (0, 4, 1, 2, 3))
    return y, state[None], k_out, v3.reshape(B, S, H_B, 2 * HD_B)


def _trunk_sample(x, state0, cache_kt, cache_v, page_table, W, lam, subln, lam_init):
    B = x.shape[0]
    x2d = x.reshape(B, D_MODEL)
    qs, k, lf, v, gs = _inproj(x2d, W['w_in'], W['lb'], 0, B)
    o, state = _hgrn_step(state0, qs, k, lf, v)
    x1, x1g, r_idx, r_gate = _mix0(o, gs, x2d, W, 0)
    yk, gates = _moe(x1g, r_idx, r_gate, 0, W, 32)
    x2, k_sh, v_sh, q = _ffn_out(x1, yk, gates, W['ln_ffn_g'][0], W['ln_ffn_b'][0],
                                 [W['w_k'], W['w_v'], W['w_q']])
    d = _attn_paged(q, k_sh, v_sh, cache_kt, cache_v, page_table, lam, subln, lam_init)
    x3, x3g, r_idx, r_gate = _mix1(d, x2, W)
    yk, gates = _moe(x3g, r_idx, r_gate, 1, W, 32)
    (x4,) = _ffn_out(x3, yk, gates, W['ln_ffn_g'][1], W['ln_ffn_b'][1], [])
    return (x4.reshape(B, 1, D_MODEL), state[None],
            k_sh.reshape(B, 1, H_B, 2, HD_B), v_sh.reshape(B, 1, H_B, 2 * HD_B))


def kernel(x_prompt, x_sample, state_hgrn, cache_k, cache_v, page_table, w_in_a, lb_logits, gnorm_a, w_o_a,
           w_kv, w_q_b, lam_q1, lam_k1, lam_q2, lam_k2, subln_b, w_o_b, ln_mix_g, ln_mix_b, ln_ffn_g, ln_ffn_b,
           w_router, b_router, w1, b1, w2, b2):
    q_dim = H_B * 2 * HD_B
    lb = jnp.cumsum(jax.nn.softmax(lb_logits.astype(f32), axis=0), axis=0)[:N_A_LAYERS]
    W = dict(
        lb=lb[0].reshape(1, D_A), w_in=w_in_a[0].astype(bf16), gnorm=gnorm_a[0], w_o_a=w_o_a[0].astype(bf16),
        w_k=w_kv[:, :q_dim].astype(bf16), w_v=w_kv[:, q_dim:].astype(bf16), w_q=w_q_b[0].astype(bf16),
        w_q_scaled=(w_q_b[0] * (HD_B ** -0.5)).astype(bf16),
        w_o_b=w_o_b[0].astype(bf16), ln_mix_g=ln_mix_g, ln_mix_b=ln_mix_b, ln_ffn_g=ln_ffn_g, ln_ffn_b=ln_ffn_b,
        w_router=w_router, b_router=b_router, w1=w1, b1=b1, w2=w2, b2=b2)
    layer = N_A_LAYERS
    lam_init = 0.8 - 0.6 * math.exp(-0.3 * layer)
    lam = (jnp.exp(jnp.sum((lam_q1[0] * lam_k1[0]).astype(f32)))
           - jnp.exp(jnp.sum((lam_q2[0] * lam_k2[0]).astype(f32))) + lam_init)
    subln = subln_b[0]

    y_p, st_p, k_p, v_p = _trunk_prompt(x_prompt, W, lam, subln, lam_init)

    n_pool = cache_k.shape[0]
    cache_kt = jnp.transpose(cache_k, (0, 2, 3, 4, 1)).reshape(n_pool, D_MODEL, PAGE_SIZE)
    y_s, st_s, k_s, v_s = _trunk_sample(x_sample, state_hgrn[0], cache_kt, cache_v, page_table, W,
                                        lam, subln, lam_init)
    return (y_p, y_s, st_p, st_s, k_p, v_p, k_s, v_s)
```

```python
import functools
import math

import numpy as np
import jax
import jax.numpy as jnp
from jax import lax
from jax.experimental import pallas as pl
from jax.experimental.pallas import tpu as pltpu

f32 = jnp.float32
bf16 = jnp.bfloat16

D_MODEL = 1024
DEPTH = 2
N_A_LAYERS = 1
H_A = 8
DK_A = 128
DV_A = 128
D_A = H_A * DK_A
HD_B = 64
H_B = 8
N_EXPERTS = 32
TOP_K = 4
D_FF = D_MODEL
SWIGLU_LIMIT = 7.0
SWIGLU_ALPHA = 1.702
DEEPNORM_ALPHA = (2.0 * DEPTH) ** 0.25
LN_EPS = 1e-5
NEG_INF = -1e30
PAGE_SIZE = 128
LOG2_E = 1.4426950408889634

LANES = 128
ROW_SLAB = D_MODEL // LANES
VMEM_LIMIT = 56 * 1024 * 1024
MOE_VMEM_LIMIT = 60 * 1024 * 1024

HGRN_TILE = 128
HGRN_HEADS_PER_STEP = 2
MOE_ROWS = 512


def _cparams(sem):
    return pltpu.CompilerParams(dimension_semantics=sem, vmem_limit_bytes=VMEM_LIMIT)


def _sigmoid(x):
    return 1.0 / (1.0 + jnp.exp(-x))


def _layer_norm_rows(x, g, b):
    mu = jnp.mean(x, -1, keepdims=True)
    xc = x - mu
    var = jnp.mean(xc * xc, -1, keepdims=True)
    return xc * lax.rsqrt(var + LN_EPS) * g + b


def _dot(a, b):
    return jnp.dot(a, b, preferred_element_type=f32)


def _dot_nt(a, b):
    return lax.dot_general(a, b, (((1,), (1,)), ((), ())), preferred_element_type=f32)


def _dot_tn(a, b):
    return lax.dot_general(a, b, (((0,), (0,)), ((), ())), preferred_element_type=f32)


def _inproj_kernel(x_ref, w_ref, lb_ref, qs_ref, k_ref, lf_ref, v_ref, gs_ref):
    h = _dot(x_ref[...].astype(bf16), w_ref[...])
    q = h[:, :D_A]
    fr = h[:, D_A:2 * D_A]
    lb = lb_ref[...]
    z = jnp.exp(-jnp.abs(fr))
    r = 1.0 / (1.0 + z)
    zr = z * r
    pos = fr >= 0
    sig = jnp.where(pos, r, zr)
    nsig = jnp.where(pos, zr, r)
    qs_ref[...] = q * _sigmoid(q)
    k_ref[...] = (1.0 - lb) * nsig
    lf_ref[...] = jnp.log(lb + (1.0 - lb) * sig)
    v_ref[...] = h[:, 2 * D_A:3 * D_A]
    g = h[:, 3 * D_A:]
    gs_ref[...] = g * _sigmoid(g)


def _inproj(x2d, w_in, lb, row0, T):
    tm = min(256, T)
    blk0 = row0 // tm
    out = jax.ShapeDtypeStruct((T, D_A), f32)
    row = pl.BlockSpec((tm, D_A), lambda i: (i, 0))
    return pl.pallas_call(
        _inproj_kernel,
        grid=(T // tm,),
        in_specs=[pl.BlockSpec((tm, D_MODEL), lambda i: (i + blk0, 0)),
                  pl.BlockSpec((D_MODEL, 4 * D_A), lambda i: (0, 0)),
                  pl.BlockSpec((1, D_A), lambda i: (0, 0))],
        out_specs=[row] * 5,
        out_shape=[out] * 5,
        compiler_params=_cparams(("parallel",)),
        name="hgrn_inproj",
    )(x2d, w_in, lb)


def _hgrn_levels(R):
    n = R // 2
    out = []
    while n >= 1:
        out.append(n)
        n //= 2
    return out


def _hgrn_masks(R):
    t = np.arange(R)[:, None]
    s = np.arange(R)[None, :]
    ms = []
    for n in _hgrn_levels(R):
        ms.append((t // (2 * n) == s // (2 * n)) & (t % (2 * n) >= n) & (s % (2 * n) < n))
    ms.append(t == s)
    return np.stack(ms).astype(np.float32)


def _level_ref_rows(b, n):
    R = b.shape[0]
    if 2 * n >= 8:
        b3 = b.reshape(R // (2 * n), 2 * n, LANES)
        return jnp.broadcast_to(b3[:, n - 1:n, :], b3.shape).reshape(R, LANES)
    b3 = b.reshape(R // 8, 8, LANES)
    sub = lax.broadcasted_iota(jnp.int32, b3.shape, 1)
    out = None
    for r0 in range(0, 8, 2 * n):
        row = jnp.broadcast_to(b3[:, r0 + n - 1:r0 + n, :], b3.shape)
        out = row if out is None else jnp.where(sub >= r0, row, out)
    return out.reshape(R, LANES)


def _split3(x):
    hi = x.astype(bf16)
    r1 = x - hi.astype(f32)
    mid = r1.astype(bf16)
    lo = (r1 - mid.astype(f32)).astype(bf16)
    return hi, mid, lo


def _hgrn_seq_kernel(qs_ref, k_ref, lf_ref, v_ref, masks_ref, tril_ref, o_ref, st_ref, state_t):
    R = HGRN_TILE
    levels = _hgrn_levels(R)
    step = pl.program_id(2)

    @pl.when(step == 0)
    def _():
        state_t[...] = jnp.zeros_like(state_t)

    def tile_head(r, hh):
        cols = slice(hh * LANES, (hh + 1) * LANES)
        q = qs_ref[pl.ds(r, R), cols]
        k = k_ref[pl.ds(r, R), cols]
        v = v_ref[pl.ds(r, R), cols].astype(bf16)
        hi, mid, lo = _split3(lf_ref[pl.ds(r, R), cols] * LOG2_E)
        c3 = _dot(tril_ref[...], jnp.concatenate([hi, mid, lo], axis=1))
        b = c3[:, :LANES] + c3[:, LANES:2 * LANES] + c3[:, 2 * LANES:]
        a = masks_ref[len(levels)] * _dot_nt(q.astype(bf16), k.astype(bf16))
        for li, n in enumerate(levels):
            e = jnp.exp2(-jnp.abs(b - _level_ref_rows(b, n)))
            a = a + masks_ref[li] * _dot_nt((q * e).astype(bf16), (k * e).astype(bf16))
        st = state_t[hh]
        b_last = b[R - 1:R, :]
        qd = (q * jnp.exp2(b)).astype(bf16)
        kd = (k * jnp.exp2(b_last - b)).astype(bf16)
        o_ref[pl.ds(r, R), cols] = _dot(a.astype(bf16), v) + _dot_nt(qd, st.astype(bf16))
        state_t[hh] = st * jnp.exp2(b_last) + _dot_tn(v, kd)

    def tile(i, carry):
        r = pl.multiple_of(i * R, R)
        for hh in range(HGRN_HEADS_PER_STEP):
            tile_head(r, hh)
        return carry

    lax.fori_loop(0, qs_ref.shape[0] // R, tile, 0)

    @pl.when(step == pl.num_programs(2) - 1)
    def _():
        st_ref[0] = state_t[...]


def _hgrn_seq(qs, k, lf, v, B, S):
    R = HGRN_TILE
    G = HGRN_HEADS_PER_STEP
    rb = min(512, S)
    nst = S // rb
    masks = jnp.asarray(_hgrn_masks(R))
    tril = jnp.asarray(np.tril(np.ones((R, R), np.float32)), dtype=bf16)
    row = pl.BlockSpec((rb, G * LANES), lambda b, h, j: (b * nst + j, h))
    o, st_t = pl.pallas_call(
        _hgrn_seq_kernel,
        grid=(B, H_A // G, nst),
        in_specs=[row, row, row, row,
                  pl.BlockSpec(masks.shape, lambda b, h, j: (0, 0, 0)),
                  pl.BlockSpec((R, R), lambda b, h, j: (0, 0))],
        out_specs=[row, pl.BlockSpec((1, G, DV_A, DK_A), lambda b, h, j: (b, h, 0, 0))],
        out_shape=[jax.ShapeDtypeStruct((B * S, D_A), f32),
                   jax.ShapeDtypeStruct((B, H_A, DV_A, DK_A), f32)],
        scratch_shapes=[pltpu.VMEM((G, DV_A, DK_A), f32)],
        compiler_params=_cparams(("parallel", "parallel", "arbitrary")),
        name="hgrn_seq",
    )(qs, k, lf, v, masks, tril)
    return o, jnp.swapaxes(st_t, 2, 3)


def _hgrn_step_kernel(s_ref, q_ref, k_ref, lf_ref, v_ref, o_ref, so_ref):
    for h in range(H_A):
        s_new = jnp.exp(lf_ref[0, h]) * s_ref[0, h] + k_ref[0, h] * v_ref[0, h]
        so_ref[0, h] = s_new
        o_ref[0, h] = jnp.sum(q_ref[0, h] * s_new, axis=0, keepdims=True)


def _hgrn_step(state, qs, k, lf, v):
    B = state.shape[0]
    col = lambda t: t.reshape(B, H_A, DK_A, 1)
    cspec = pl.BlockSpec((1, H_A, DK_A, 1), lambda b: (b, 0, 0, 0))
    rspec = pl.BlockSpec((1, H_A, 1, DV_A), lambda b: (b, 0, 0, 0))
    sspec = pl.BlockSpec((1, H_A, DK_A, DV_A), lambda b: (b, 0, 0, 0))
    o, s_new = pl.pallas_call(
        _hgrn_step_kernel,
        grid=(B,),
        in_specs=[sspec, cspec, cspec, cspec, rspec],
        out_specs=[rspec, sspec],
        out_shape=[jax.ShapeDtypeStruct((B, H_A, 1, DV_A), f32),
                   jax.ShapeDtypeStruct(state.shape, f32)],
        compiler_params=_cparams(("parallel",)),
        name="hgrn_step",
    )(state, col(qs), col(k), col(lf), v.reshape(B, H_A, 1, DV_A))
    return o.reshape(B, D_A), s_new


def _mix_out_kernel(gated, *refs):
    if gated:
        m_ref, gs_ref, gn_ref, x_ref, wo_ref, lg_ref, lbias_ref, wr_ref, br_ref, x1_ref, xg_ref, idx_ref, gate_ref = refs
        m = m_ref[...] * gs_ref[...]
        m = m * lax.rsqrt(jnp.mean(m * m, -1, keepdims=True) + LN_EPS) * gn_ref[...]
        m = m.astype(bf16)
    else:
        m_ref, x_ref, wo_ref, lg_ref, lbias_ref, wr_ref, br_ref, x1_ref, xg_ref, idx_ref, gate_ref = refs
        m = m_ref[...]
    mix = _dot(m, wo_ref[...])
    x1 = _layer_norm_rows(DEEPNORM_ALPHA * x_ref[...] + mix, lg_ref[...], lbias_ref[...])
    x1_ref[...] = x1
    for j in range(ROW_SLAB):
        xg_ref[:, j, :] = x1[:, j * LANES:(j + 1) * LANES]
    logits =jnp.dot(x1, wr_ref[...], preferred_element_type=f32,
                     precision=lax.Precision.HIGHEST) + br_ref[...]
    lane = lax.broadcasted_iota(jnp.int32, logits.shape, 1).astype(f32)
    cur = logits
    idx_out = jnp.zeros_like(logits)
    gate_out = jnp.zeros_like(logits)
    denom = None
    top = None
    for kk in range(TOP_K):
        v = jnp.max(cur, -1, keepdims=True)
        idx = jnp.min(jnp.where(cur == v, lane, float(N_EXPERTS)), -1, keepdims=True)
        top = v if top is None else top
        e = jnp.exp(v - top)
        denom = e if denom is None else denom + e
        idx_out = jnp.where(lane == kk, idx, idx_out)
        gate_out = jnp.where(lane == kk, e, gate_out)
        cur = jnp.where(lane == idx, -jnp.inf, cur)
    idx_ref[...] = idx_out.astype(jnp.int32)
    gate_ref[...] = gate_out / denom


def _mix_out(m, gate, x2d, w_o, ln_g, ln_b, w_router, b_router, row0=0):
    T = m.shape[0]
    tm = min(256, T)
    blk0 = row0 // tm
    row = pl.BlockSpec((tm, D_MODEL), lambda i: (i, 0))
    vec = pl.BlockSpec((1, D_MODEL), lambda i: (0, 0))
    gated = gate is not None
    ins = [m] + ([gate[0], gate[1].reshape(1, -1)] if gated else []) + [
        x2d, w_o, ln_g.reshape(1, -1), ln_b.reshape(1, -1), w_router, b_router.reshape(1, -1)]
    specs = [row] + ([row, vec] if gated else []) + [
        pl.BlockSpec((tm, D_MODEL), lambda i: (i + blk0, 0)),
        pl.BlockSpec((D_MODEL, D_MODEL), lambda i: (0, 0)), vec, vec,
        pl.BlockSpec((D_MODEL, N_EXPERTS), lambda i: (0, 0)),
        pl.BlockSpec((1, N_EXPERTS), lambda i: (0, 0))]
    return pl.pallas_call(
        functools.partial(_mix_out_kernel, gated),
        grid=(T // tm,),
        in_specs=specs,
        out_specs=[row, pl.BlockSpec((tm, ROW_SLAB, LANES), lambda i: (i, 0, 0))]
                  + [pl.BlockSpec((tm, N_EXPERTS), lambda i: (i, 0))] * 2,
        out_shape=[jax.ShapeDtypeStruct((T, D_MODEL), f32),
                   jax.ShapeDtypeStruct((T, ROW_SLAB, LANES), f32),
                   jax.ShapeDtypeStruct((T, N_EXPERTS), jnp.int32),
                   jax.ShapeDtypeStruct((T, N_EXPERTS), f32)],
        compiler_params=_cparams(("parallel",)),
        name="mix_out_gated" if gated else "mix_out",
    )(*ins)


MOE_FF_CHUNK = 512
MOE_ISSUE_PARTS = 2 * (D_FF // MOE_FF_CHUNK)


def _moe_kernel(blk_e_ref, n_used_ref, tok_ref, x_hbm, w1_ref, b1_ref, w2_ref, b2_ref, y_ref,
                w1b, w2b, xbuf, sem):
    i = pl.program_id(0)
    bm = y_ref.shape[0]
    n_used = n_used_ref[0]
    live = i < n_used
    slot = lax.rem(i, 2)

    def row_copy(src_row, dst_row, s):
        return pltpu.make_async_copy(x_hbm.at[pl.ds(src_row, ROW_SLAB)],
                                     xbuf.at[s, pl.ds(dst_row, ROW_SLAB)], sem.at[s])

    def start_rows(blk, s, r0, n):
        def body(r, carry):
            tok = tok_ref[blk * bm + r]
            row_copy(pl.multiple_of(tok * ROW_SLAB, ROW_SLAB), pl.multiple_of(r * ROW_SLAB, ROW_SLAB), s).start()
            return carry
        lax.fori_loop(r0, r0 + n, body, 0, unroll=8)

    def prefetch_part(p):
        n = bm // MOE_ISSUE_PARTS

        @pl.when(i + 1 < n_used)
        def _():
            start_rows(i + 1, 1 - slot, p * n, n)

    @pl.when(live & (i == 0))
    def _():
        start_rows(0, 0, 0, bm)

    prev = blk_e_ref[jnp.maximum(i - 1, 0)]

    @pl.when(live & ((i == 0) | (blk_e_ref[i] != prev)))
    def _():
        w1b[...] = w1_ref[0, 0].astype(bf16)
        w2b[...] = w2_ref[0, 0].astype(bf16)

    @pl.when(live)
    def _():
        pltpu.make_async_copy(x_hbm.at[pl.ds(0, bm * ROW_SLAB)], xbuf.at[slot], sem.at[slot]).wait()
        x = jnp.concatenate([xbuf[slot, pl.ds(j, bm, stride=ROW_SLAB), :] for j in range(ROW_SLAB)],
                            axis=1).astype(bf16)
        y = None
        for ci, c in enumerate(range(0, D_FF, MOE_FF_CHUNK)):
            prefetch_part(2 * ci)
            hg = _dot(x, w1b[:, c:c + MOE_FF_CHUNK]) + b1_ref[0, 0, :, c:c + MOE_FF_CHUNK]
            prefetch_part(2 * ci + 1)
            hl = (_dot(x, w1b[:, D_FF + c:D_FF + c + MOE_FF_CHUNK])
                  + b1_ref[0, 0, :, D_FF + c:D_FF + c + MOE_FF_CHUNK])
            x_glu = jnp.minimum(hg, SWIGLU_LIMIT)
            x_lin = jnp.clip(hl, -SWIGLU_LIMIT, SWIGLU_LIMIT)
            act = x_glu * _sigmoid(SWIGLU_ALPHA * x_glu) * (x_lin + 1.0)
            part = _dot(act.astype(bf16), w2b[c:c + MOE_FF_CHUNK, :])
            y = part if y is None else y + part
        y_ref[...] = (y + b2_ref[0, 0]).astype(y_ref.dtype)

    @pl.when(jnp.logical_not(live))
    def _():
        y_ref[...] = jnp.zeros_like(y_ref)


def _moe_rows(x_slabs, row_tok, blk_e, n_used, layer, w1, b1, w2, b2, bm):
    nb = row_tok.shape[0] // bm
    grid_spec = pltpu.PrefetchScalarGridSpec(
        num_scalar_prefetch=3,
        grid=(nb,),
        in_specs=[pl.BlockSpec(memory_space=pl.ANY),
                  pl.BlockSpec((1, 1, D_MODEL, 2 * D_FF), lambda i, e, n, t: (layer, e[i], 0, 0)),
                  pl.BlockSpec((1, 1, 1, 2 * D_FF), lambda i, e, n, t: (layer, e[i], 0, 0)),
                  pl.BlockSpec((1, 1, D_FF, D_MODEL), lambda i, e, n, t: (layer, e[i], 0, 0)),
                  pl.BlockSpec((1, 1, 1, D_MODEL), lambda i, e, n, t: (layer, e[i], 0, 0))],
        out_specs=pl.BlockSpec((bm, D_MODEL), lambda i, e, n, t: (i, 0)),
        scratch_shapes=[pltpu.VMEM((D_MODEL, 2 * D_FF), bf16), pltpu.VMEM((D_FF, D_MODEL), bf16),
                        pltpu.VMEM((2, bm * ROW_SLAB, LANES), f32), pltpu.SemaphoreType.DMA((2,))],
    )
    return pl.pallas_call(
        _moe_kernel,
        grid_spec=grid_spec,
        out_shape=jax.ShapeDtypeStruct((nb * bm, D_MODEL), bf16),
        compiler_params=pltpu.CompilerParams(dimension_semantics=("arbitrary",),
                                             vmem_limit_bytes=MOE_VMEM_LIMIT),
        name="moe_experts",
    )(blk_e, n_used, row_tok, x_slabs, w1, b1.reshape(DEPTH, N_EXPERTS, 1, -1), w2,
      b2.reshape(DEPTH, N_EXPERTS, 1, -1))


def _moe_ffn(x_slabs, route_idx, route_gate, layer, w1, b1, w2, b2, bm):
    T = x_slabs.shape[0]
    n_slots = T * TOP_K
    top_idx = route_idx[:, :TOP_K]
    gates = route_gate[:, :TOP_K]
    flat_e = top_idx.reshape(-1)
    order = jnp.argsort(flat_e)
    inv = jnp.argsort(order)
    counts = jnp.bincount(flat_e, length=N_EXPERTS)
    padded = (counts + bm - 1) // bm * bm
    pad_end = jnp.cumsum(padded)
    pad_start = pad_end - padded
    start = jnp.cumsum(counts) - counts
    n_blocks = (n_slots + N_EXPERTS * (bm - 1) + bm - 1) // bm
    blk_e = jnp.minimum(jnp.sum(pad_end[None, :] <= (jnp.arange(n_blocks) * bm)[:, None], axis=1),
                        N_EXPERTS - 1).astype(jnp.int32)
    n_used = (pad_end[-1] // bm).astype(jnp.int32).reshape(1)
    blk = jnp.arange(n_blocks)
    first = start[blk_e] + blk * bm - pad_start[blk_e]
    last = jnp.where(blk < n_used[0], start[blk_e] + counts[blk_e], 0)
    src = first[:, None] + jnp.arange(bm)[None, :]
    row_tok = jnp.where(src < last[:, None], order[jnp.clip(src, 0, n_slots - 1)] // TOP_K, 0)
    row_tok = row_tok.reshape(-1).astype(jnp.int32)
    y_rows = _moe_rows(x_slabs.reshape(T * ROW_SLAB, LANES), row_tok, blk_e, n_used, layer, w1, b1, w2, b2, bm)
    pos = ((pad_start - start)[flat_e] + inv).astype(jnp.int32)
    pos_k = pos.reshape(T, TOP_K).T.reshape(-1)
    return y_rows[pos_k].reshape(TOP_K, T, D_MODEL), gates


def _ffn_combine(x_ref, y_refs, gt_ref, g_ref, b_ref):
    gt = gt_ref[...]
    y = y_refs[0][0].astype(f32) * gt[:, 0:1]
    for kk in range(1, TOP_K):
        y = y + y_refs[kk][0].astype(f32) * gt[:, kk:kk + 1]
    return _layer_norm_rows(DEEPNORM_ALPHA * x_ref[...] + y, g_ref[...], b_ref[...])


def _ffn_out_kernel(n_proj, x_ref, y0, y1, y2, y3, gt_ref, g_ref, b_ref, *rest):
    x2 = _ffn_combine(x_ref, (y0, y1, y2, y3), gt_ref, g_ref, b_ref)
    rest[n_proj][...] = x2
    xb = x2.astype(bf16)
    for p in range(n_proj):
        rest[n_proj + 1 + p][...] = _dot(xb, rest[p][...])


def _ffn_in_specs(tm, row_index):
    row = pl.BlockSpec((tm, D_MODEL), lambda *g: (row_index(*g), 0))
    yk = [pl.BlockSpec((1, tm, D_MODEL), functools.partial(lambda kk, *g: (kk, row_index(*g), 0), kk))
          for kk in range(TOP_K)]
    vec = pl.BlockSpec((1, D_MODEL), lambda *g: (0, 0))
    return row, [row] + yk + [pl.BlockSpec((tm, TOP_K), lambda *g: (row_index(*g), 0)), vec, vec]


def _ffn_out(x1, yk, gates, ln_g, ln_b, proj_ws):
    T = x1.shape[0]
    tm = min(256, T)
    row, specs = _ffn_in_specs(tm, lambda i: i)
    n_proj = len(proj_ws)
    return pl.pallas_call(
        functools.partial(_ffn_out_kernel, n_proj),
        grid=(T // tm,),
        in_specs=specs + [pl.BlockSpec(w.shape, lambda i: (0, 0)) for w in proj_ws],
        out_specs=[row] + [pl.BlockSpec((tm, w.shape[1]), lambda i: (i, 0)) for w in proj_ws],
        out_shape=[jax.ShapeDtypeStruct((T, D_MODEL), f32)] +
                  [jax.ShapeDtypeStruct((T, w.shape[1]), f32) for w in proj_ws],
        compiler_params=_cparams(("parallel",)),
        name="ffn_out_%d" % n_proj,
    )(x1, yk, yk, yk, yk, gates, ln_g.reshape(1, -1), ln_b.reshape(1, -1), *proj_ws)


def _ffn_out_seq_kernel(x_ref, y0, y1, y2, y3, gt_ref, g_ref, b_ref, wk_ref, wkt_ref, wv_ref, wvt_ref, wqt_ref,
                        x2_ref, kt_ref, v3_ref, kb_ref, vt_ref, qt_ref):
    x2 = _ffn_combine(x_ref, (y0, y1, y2, y3), gt_ref, g_ref, b_ref)
    x2_ref[...] = x2
    xb = x2.astype(bf16)
    kb_ref[...] = _dot(xb, wk_ref[...]).astype(bf16)
    kt_ref[0] = _dot_nt(wkt_ref[...], xb)
    vt_ref[0] = _dot_nt(wvt_ref[...], xb).astype(bf16)
    qt_ref[0] = _dot_nt(wqt_ref[...], xb).astype(bf16)
    v = _dot(xb, wv_ref[...])
    for h in range(H_B):
        v3_ref[:, h, :] = v[:, h * LANES:(h + 1) * LANES]


def _ffn_out_seq(x1, yk, gates, ln_g, ln_b, W, B, S):
    T = B * S
    tm = 256
    nj = S // tm
    row, specs = _ffn_in_specs(tm, lambda b, j: b * nj + j)
    wspec = pl.BlockSpec((D_MODEL, D_MODEL), lambda b, j: (0, 0), pipeline_mode=pl.Buffered(1))
    tspec = pl.BlockSpec((1, D_MODEL, tm), lambda b, j: (b, 0, j))
    return pl.pallas_call(
        _ffn_out_seq_kernel,
        grid=(B, nj),
        in_specs=specs + [wspec] * 5,
        out_specs=[row, tspec, pl.BlockSpec((tm, H_B, LANES), lambda b, j: (b * nj + j, 0, 0)), row, tspec, tspec],
        out_shape=[jax.ShapeDtypeStruct((T, D_MODEL), f32),
                   jax.ShapeDtypeStruct((B, D_MODEL, S), f32),
                   jax.ShapeDtypeStruct((T, H_B, LANES), f32),
                   jax.ShapeDtypeStruct((T, D_MODEL), bf16),
                   jax.ShapeDtypeStruct((B, D_MODEL, S), bf16),
                   jax.ShapeDtypeStruct((B, D_MODEL, S), bf16)],
        compiler_params=_cparams(("parallel", "parallel")),
        name="ffn_out_seq",
    )(x1, yk, yk, yk, yk, gates, ln_g.reshape(1, -1), ln_b.reshape(1, -1),
      W['w_k'], W['w_k'].T, W['w_v'], W['w_v'].T, W['w_q_scaled'].T)


ATT_T = 512


def _attn_seq_kernel(lam_init, slopes_ref, lam_ref, qt_ref, k_ref, vt_ref, g_ref, o_ref,
                     qt2_scr, bias_scr, bias_diag_scr, acc_scr):
    t = ATT_T
    h = pl.program_id(1)
    i = pl.program_id(2)
    slope = slopes_ref[h]

    @pl.when(i == 0)
    def _():
        kk = lax.broadcasted_iota(jnp.int32, (t, 2 * t), 0)
        qq = lax.broadcasted_iota(jnp.int32, (t, 2 * t), 1)
        qq = jnp.where(qq >= t, qq - t, qq)
        rel = slope * (kk - qq).astype(f32)
        bias_scr[...] = rel
        bias_diag_scr[...] = jnp.where(kk <= qq, rel, NEG_INF)

    qt = qt_ref[0]
    dim = lax.broadcasted_iota(jnp.int32, qt.shape, 0)
    zero = jnp.zeros_like(qt)
    qt2_scr[:, :t] = jnp.where(dim < HD_B, qt, zero)
    qt2_scr[:, t:] = jnp.where(dim >= HD_B, qt, zero)
    acc_scr[...] = jnp.zeros_like(acc_scr)

    def kv_block(j, bias_ref, m_old, l_old):
        c = pl.multiple_of(j * t, t)
        s = _dot(k_ref[pl.ds(c, t), :], qt2_scr[...]) + bias_ref[...]
        shift = slope * ((i - j) * t).astype(f32)
        m_new = jnp.maximum(m_old, jnp.max(s, 0, keepdims=True) - shift)
        p = jnp.exp(s - (m_new + shift))
        alpha = jnp.exp(m_old - m_new)
        l_new = alpha * l_old + jnp.sum(p, 0, keepdims=True)
        acc_scr[...] = alpha * acc_scr[...] + _dot(vt_ref[0, :, pl.ds(c, t)], p.astype(bf16))
        return m_new, l_new

    m0 = jnp.full((1, 2 * t), NEG_INF, f32)
    l0 = jnp.zeros((1, 2 * t), f32)
    m, l = lax.fori_loop(0, i, lambda j, ml: kv_block(j, bias_scr, *ml), (m0, l0))
    m, l = kv_block(i, bias_diag_scr, m, l)

    o = acc_scr[...] / l
    d = o[:, :t] - lam_ref[0] * o[:, t:]
    d = d * lax.rsqrt(jnp.mean(d * d, 0, keepdims=True) + LN_EPS) * g_ref[...]
    o_ref[...] = (d * (1.0 - lam_init)).T.astype(o_ref.dtype)


def _attn_seq(qt, kb, vt, lam, subln, lam_init, B, S):
    t = ATT_T
    nq = S // t
    slopes = jnp.exp2(-8.0 * jnp.arange(1, H_B + 1, dtype=f32) / H_B)
    smem = pl.BlockSpec(memory_space=pltpu.SMEM)
    return pl.pallas_call(
        functools.partial(_attn_seq_kernel, lam_init),
        grid=(B, H_B, nq),
        in_specs=[smem, smem,
                  pl.BlockSpec((1, LANES, t), lambda b, h, i: (b, h, i)),
                  pl.BlockSpec((S, LANES), lambda b, h, i: (b, h)),
                  pl.BlockSpec((1, LANES, S), lambda b, h, i: (b, h, 0)),
                  pl.BlockSpec((LANES, 1), lambda b, h, i: (0, 0))],
        out_specs=pl.BlockSpec((t, LANES), lambda b, h, i: (b * nq + i, h)),
        out_shape=jax.ShapeDtypeStruct((B * S, D_MODEL), bf16),
        scratch_shapes=[pltpu.VMEM((LANES, 2 * t), bf16),
                        pltpu.VMEM((t, 2 * t), f32),
                        pltpu.VMEM((t, 2 * t), f32),
                        pltpu.VMEM((LANES, 2 * t), f32)],
        compiler_params=_cparams(("parallel", "arbitrary", "arbitrary")),
        name="attn_seq",
    )(slopes, lam.reshape(1), qt, kb, vt, subln.reshape(-1, 1))


PAGES_PER_STEP = 8


def _attn_paged_kernel(lam_init, n_pages, pt_ref, slopes_ref, lam_ref, q_ref, kn_ref, vn_ref, g_ref, *rest):
    G = PAGES_PER_STEP
    k_refs = rest[:G]
    v_refs = rest[G:2 * G]
    o_ref, qbd_scr, m_scr, l_scr, acc_scr = rest[2 * G:]
    step = pl.program_id(1)
    nrow = 2 * H_B
    r1 = lax.broadcasted_iota(jnp.int32, (nrow, 1), 0)
    slope_col = jnp.zeros((nrow, 1), f32)
    for hh in range(H_B):
        slope_col = jnp.where(r1 // 2 == hh, slopes_ref[hh], slope_col)

    @pl.when(step == 0)
    def _():
        rowi = lax.broadcasted_iota(jnp.int32, (nrow, D_MODEL), 0)
        coli = lax.broadcasted_iota(jnp.int32, (nrow, D_MODEL), 1)
        own = (coli // HD_B) == rowi
        q = q_ref[0] * (HD_B ** -0.5)
        qbd = jnp.where(own, jnp.broadcast_to(q, (nrow, D_MODEL)), 0.0).astype(bf16)
        qbd_scr[...] = qbd
        m_scr[...] = jnp.sum(qbd.astype(f32) * kn_ref[0].astype(bf16).astype(f32), -1, keepdims=True)
        l_scr[...] = jnp.ones_like(l_scr)
        acc_scr[...] = jnp.broadcast_to(vn_ref[0].astype(bf16).astype(f32), (nrow, D_MODEL))

    s = jnp.concatenate([_dot(qbd_scr[...], k_refs[u][0].astype(bf16)) for u in range(G)], axis=1)
    tok = lax.broadcasted_iota(jnp.int32, s.shape, 1)
    dist = ((n_pages - step * G) * PAGE_SIZE - tok).astype(f32)
    s = s - slope_col * dist
    m = m_scr[...]
    m_new = jnp.maximum(m, jnp.max(s, -1, keepdims=True))
    p32 = jnp.exp(s - m_new)
    alpha = jnp.exp(m - m_new)
    l = alpha * l_scr[...] + jnp.sum(p32, -1, keepdims=True)
    p = p32.astype(bf16)
    acc = alpha * acc_scr[...]
    for u in range(G):
        vb = jnp.concatenate([v_refs[u][0, pl.ds(hh, PAGE_SIZE, stride=H_B), :] for hh in range(H_B)],
                             axis=1).astype(bf16)
        acc = acc + _dot(p[:, u * PAGE_SIZE:(u + 1) * PAGE_SIZE], vb)
    m_scr[...] = m_new
    l_scr[...] = l
    acc_scr[...] = acc

    @pl.when(step == pl.num_programs(1) - 1)
    def _():
        o = acc / l
        for hh in range(H_B):
            blk = o[2 * hh:2 * hh + 2, hh * LANES:(hh + 1) * LANES]
            d = blk[0:1, :] - lam_ref[0] * blk[1:2, :]
            d = d * lax.rsqrt(jnp.mean(d * d, -1, keepdims=True) + LN_EPS) * g_ref[...]
            o_ref[0, :, hh * LANES:(hh + 1) * LANES] = (d * (1.0 - lam_init)).astype(o_ref.dtype)


def _attn_paged(q, k_new, v_new, cache_kt, cache_v, page_table, lam, subln, lam_init):
    B = q.shape[0]
    G = PAGES_PER_STEP
    n_pages = page_table.shape[1]
    slopes = jnp.exp2(-8.0 * jnp.arange(1, H_B + 1, dtype=f32) / H_B)
    smem = pl.BlockSpec(memory_space=pltpu.SMEM)
    vec = pl.BlockSpec((1, 1, D_MODEL), lambda b, s, pt: (b, 0, 0))

    def page_index(u, ndim):
        return lambda b, s, pt: (pt[b * n_pages + s * G + u],) + (0,) * (ndim - 1)

    k_specs = [pl.BlockSpec((1, D_MODEL, PAGE_SIZE), page_index(u, 3)) for u in range(G)]
    v_specs = [pl.BlockSpec((1, PAGE_SIZE * H_B, LANES), page_index(u, 3)) for u in range(G)]
    grid_spec = pltpu.PrefetchScalarGridSpec(
        num_scalar_prefetch=1,
        grid=(B, n_pages // G),
        in_specs=[smem, smem, vec, vec, vec, pl.BlockSpec((1, LANES), lambda b, s, pt: (0, 0))] + k_specs + v_specs,
        out_specs=vec,
        scratch_shapes=[pltpu.VMEM((2 * H_B, D_MODEL), bf16),
                        pltpu.VMEM((2 * H_B, 1), f32),
                        pltpu.VMEM((2 * H_B, 1), f32),
                        pltpu.VMEM((2 * H_B, D_MODEL), f32)],
    )
    d = pl.pallas_call(
        functools.partial(_attn_paged_kernel, lam_init, n_pages),
        grid_spec=grid_spec,
        out_shape=jax.ShapeDtypeStruct((B, 1, D_MODEL), bf16),
        compiler_params=_cparams(("parallel", "arbitrary")),
        name="attn_paged",
    )(page_table.reshape(-1), slopes, lam.reshape(1), q.reshape(B, 1, -1), k_new.reshape(B, 1, -1),
      v_new.reshape(B, 1, -1), subln.reshape(1, -1), *([cache_kt] * G),
      *([cache_v.reshape(-1, PAGE_SIZE * H_B, LANES)] * G))
    return d.reshape(B, D_MODEL)


def _mix0(o, gs, x2d, W, row0):
    return _mix_out(o, (gs, W['gnorm']), x2d, W['w_o_a'], W['ln_mix_g'][0], W['ln_mix_b'][0],
                    W['w_router'][0], W['b_router'][0], row0)


def _mix1(d, x2, W):
    return _mix_out(d, None, x2, W['w_o_b'], W['ln_mix_g'][1], W['ln_mix_b'][1],
                    W['w_router'][1], W['b_router'][1])


def _moe(x, route_idx, route_gate, layer, W, moe_rows):
    return _moe_ffn(x, route_idx, route_gate, layer, W['w1'], W['b1'], W['w2'], W['b2'], moe_rows)


PROMPT_CHUNKS = 2


def _trunk_prompt(x, W, lam, subln, lam_init):
    B, S, _ = x.shape
    bc = B // PROMPT_CHUNKS
    tc = bc * S
    cs = range(PROMPT_CHUNKS)
    x2d = x.reshape(B * S, D_MODEL)
    pj = [_inproj(x2d, W['w_in'], W['lb'], c * tc, tc) for c in cs]
    hg = [_hgrn_seq(*pj[c][:4], bc, S) for c in cs]
    m0 = [_mix0(hg[c][0], pj[c][4], x2d, W, c * tc) for c in cs]
    e0 = [_moe(*m0[c][1:], 0, W, MOE_ROWS) for c in cs]
    f0 = [_ffn_out_seq(m0[c][0], e0[c][0], e0[c][1], W['ln_ffn_g'][0], W['ln_ffn_b'][0], W, bc, S) for c in cs]
    at = [_attn_seq(f0[c][5], f0[c][3], f0[c][4], lam, subln, lam_init, bc, S) for c in cs]
    m1 = [_mix1(at[c], f0[c][0], W) for c in cs]
    e1 = [_moe(*m1[c][1:], 1, W, MOE_ROWS) for c in cs]
    x4 = [_ffn_out(m1[c][0], e1[c][0], e1[c][1], W['ln_ffn_g'][1], W['ln_ffn_b'][1], [])[0] for c in cs]
    y = jnp.concatenate(x4, axis=0).reshape(B, S, D_MODEL)
    state = jnp.concatenate([hg[c][1] for c in cs], axis=0)
    kt = jnp.concatenate([f0[c][1] for c in cs], axis=0)
    v3 = jnp.concatenate([f0[c][2] for c in cs], axis=0)
    k_out = jnp.transpose(kt.reshape(B, H_B, 2, HD_B, S), (0, 4, 1, 2, 3))
    return y, state[None], k_out, v3.reshape(B, S, H_B, 2 * HD_B)


def _trunk_sample(x, state0, cache_kt, cache_v, page_table, W, lam, subln, lam_init):
    B = x.shape[0]
    x2d = x.reshape(B, D_MODEL)
    qs, k, lf, v, gs = _inproj(x2d, W['w_in'], W['lb'], 0, B)
    o, state = _hgrn_step(state0, qs, k, lf, v)
    x1, x1g, r_idx, r_gate = _mix0(o, gs, x2d, W, 0)
    yk, gates = _moe(x1g, r_idx, r_gate, 0, W, 32)
    x2, k_sh, v_sh, q = _ffn_out(x1, yk, gates, W['ln_ffn_g'][0], W['ln_ffn_b'][0],
                                 [W['w_k'], W['w_v'], W['w_q']])
    d = _attn_paged(q, k_sh, v_sh, cache_kt, cache_v, page_table, lam, subln, lam_init)
    x3, x3g, r_idx, r_gate = _mix1(d, x2, W)
    yk, gates = _moe(x3g, r_idx, r_gate, 1, W, 32)
    (x4,) = _ffn_out(x3, yk, gates, W['ln_ffn_g'][1], W['ln_ffn_b'][1], [])
    return (x4.reshape(B, 1, D_MODEL), state[None],
            k_sh.reshape(B, 1, H_B, 2, HD_B), v_sh.reshape(B, 1, H_B, 2 * HD_B))


def kernel(x_prompt, x_sample, state_hgrn, cache_k, cache_v, page_table, w_in_a, lb_logits, gnorm_a, w_o_a,
           w_kv, w_q_b, lam_q1, lam_k1, lam_q2, lam_k2, subln_b, w_o_b, ln_mix_g, ln_mix_b, ln_ffn_g, ln_ffn_b,
           w_router, b_router, w1, b1, w2, b2):
    q_dim = H_B * 2 * HD_B
    lb = jnp.cumsum(jax.nn.softmax(lb_logits.astype(f32), axis=0), axis=0)[:N_A_LAYERS]
    W = dict(
        lb=lb[0].reshape(1, D_A), w_in=w_in_a[0].astype(bf16), gnorm=gnorm_a[0], w_o_a=w_o_a[0].astype(bf16),
        w_k=w_kv[:, :q_dim].astype(bf16), w_v=w_kv[:, q_dim:].astype(bf16), w_q=w_q_b[0].astype(bf16),
        w_q_scaled=(w_q_b[0] * (HD_B ** -0.5)).astype(bf16),
        w_o_b=w_o_b[0].astype(bf16), ln_mix_g=ln_mix_g, ln_mix_b=ln_mix_b, ln_ffn_g=ln_ffn_g, ln_ffn_b=ln_ffn_b,
        w_router=w_router, b_router=b_router, w1=w1, b1=b1, w2=w2, b2=b2)
    layer = N_A_LAYERS
    lam_init = 0.8 - 0.6 * math.exp(-0.3 * layer)
    lam = (jnp.exp(jnp.sum((lam_q1[0] * lam_k1[0]).astype(f32)))
           - jnp.exp(jnp.sum((lam_q2[0] * lam_k2[0]).astype(f32))) + lam_init)
    subln = subln_b[0]

    y_p, st_p, k_p, v_p = _trunk_prompt(x_prompt, W, lam, subln, lam_init)

    n_pool = cache_k.shape[0]
    cache_kt = jnp.transpose(cache_k, (0, 2, 3, 4, 1)).reshape(n_pool, D_MODEL, PAGE_SIZE)
    y_s, st_s, k_s, v_s = _trunk_sample(x_sample, state_hgrn[0], cache_kt, cache_v, page_table, W,
                                        lam, subln, lam_init)
    return (y_p, y_s, st_p, st_s, k_p, v_p, k_s, v_s)
```

```python
import functools
import math

import numpy as np
import jax
import jax.numpy as jnp
from jax import lax
from jax.experimental import pallas as pl
from jax.experimental.pallas import tpu as pltpu

f32 = jnp.float32
bf16 = jnp.bfloat16

D_MODEL = 1024
DEPTH = 2
N_A_LAYERS = 1
H_A = 8
DK_A = 128
DV_A = 128
D_A = H_A * DK_A
HD_B = 64
H_B = 8
N_EXPERTS = 32
TOP_K = 4
D_FF = D_MODEL
SWIGLU_LIMIT = 7.0
SWIGLU_ALPHA = 1.702
DEEPNORM_ALPHA = (2.0 * DEPTH) ** 0.25
LN_EPS = 1e-5
NEG_INF = -1e30
PAGE_SIZE = 128
LOG2_E = 1.4426950408889634

LANES = 128
ROW_SLAB = D_MODEL // LANES
VMEM_LIMIT = 56 * 1024 * 1024
MOE_VMEM_LIMIT = 60 * 1024 * 1024

HGRN_TILE = 128
HGRN_HEADS_PER_STEP = 2
MOE_ROWS = 512


def _cparams(sem):
    return pltpu.CompilerParams(dimension_semantics=sem, vmem_limit_bytes=VMEM_LIMIT)


def _sigmoid(x):
    return 1.0 / (1.0 + jnp.exp(-x))


def _layer_norm_rows(x, g, b):
    mu = jnp.mean(x, -1, keepdims=True)
    xc = x - mu
    var = jnp.mean(xc * xc, -1, keepdims=True)
    return xc * lax.rsqrt(var + LN_EPS) * g + b


def _dot(a, b):
    return jnp.dot(a, b, preferred_element_type=f32)


def _dot_nt(a, b):
    return lax.dot_general(a, b, (((1,), (1,)), ((), ())), preferred_element_type=f32)


def _dot_tn(a, b):
    return lax.dot_general(a, b, (((0,), (0,)), ((), ())), preferred_element_type=f32)


def _inproj_kernel(x_ref, w_ref, lb_ref, qs_ref, k_ref, lf_ref, v_ref, gs_ref):
    h = _dot(x_ref[...].astype(bf16), w_ref[...])
    q = h[:, :D_A]
    fr = h[:, D_A:2 * D_A]
    lb = lb_ref[...]
    z = jnp.exp(-jnp.abs(fr))
    r = 1.0 / (1.0 + z)
    zr = z * r
    pos = fr >= 0
    sig = jnp.where(pos, r, zr)
    nsig = jnp.where(pos, zr, r)
    qs_ref[...] = q * _sigmoid(q)
    k_ref[...] = (1.0 - lb) * nsig
    lf_ref[...] = jnp.log(lb + (1.0 - lb) * sig)
    v_ref[...] = h[:, 2 * D_A:3 * D_A]
    g = h[:, 3 * D_A:]
    gs_ref[...] = g * _sigmoid(g)


def _inproj(x2d, w_in, lb, row0, T):
    tm = min(256, T)
    blk0 = row0 // tm
    out = jax.ShapeDtypeStruct((T, D_A), f32)
    row = pl.BlockSpec((tm, D_A), lambda i: (i, 0))
    return pl.pallas_call(
        _inproj_kernel,
        grid=(T // tm,),
        in_specs=[pl.BlockSpec((tm, D_MODEL), lambda i: (i + blk0, 0)),
                  pl.BlockSpec((D_MODEL, 4 * D_A), lambda i: (0, 0)),
                  pl.BlockSpec((1, D_A), lambda i: (0, 0))],
        out_specs=[row] * 5,
        out_shape=[out] * 5,
        compiler_params=_cparams(("parallel",)),
        name="hgrn_inproj",
    )(x2d, w_in, lb)


def _hgrn_levels(R):
    n = R // 2
    out = []
    while n >= 1:
        out.append(n)
        n //= 2
    return out


def _hgrn_masks(R):
    t = np.arange(R)[:, None]
    s = np.arange(R)[None, :]
    ms = []
    for n in _hgrn_levels(R):
        ms.append((t // (2 * n) == s // (2 * n)) & (t % (2 * n) >= n) & (s % (2 * n) < n))
    ms.append(t == s)
    return np.stack(ms).astype(np.float32)


def _level_ref_rows(b, n):
    R = b.shape[0]
    if 2 * n >= 8:
        b3 = b.reshape(R // (2 * n), 2 * n, LANES)
        return jnp.broadcast_to(b3[:, n - 1:n, :], b3.shape).reshape(R, LANES)
    b3 = b.reshape(R // 8, 8, LANES)
    sub = lax.broadcasted_iota(jnp.int32, b3.shape, 1)
    out = None
    for r0 in range(0, 8, 2 * n):
        row = jnp.broadcast_to(b3[:, r0 + n - 1:r0 + n, :], b3.shape)
        out = row if out is None else jnp.where(sub >= r0, row, out)
    return out.reshape(R, LANES)


def _split3(x):
    hi = x.astype(bf16)
    r1 = x - hi.astype(f32)
    mid = r1.astype(bf16)
    lo = (r1 - mid.astype(f32)).astype(bf16)
    return hi, mid, lo


def _hgrn_seq_kernel(qs_ref, k_ref, lf_ref, v_ref, masks_ref, tril_ref, o_ref, st_ref, state_t):
    R = HGRN_TILE
    levels = _hgrn_levels(R)
    step = pl.program_id(2)

    @pl.when(step == 0)
    def _():
        state_t[...] = jnp.zeros_like(state_t)

    def tile_head(r, hh):
        cols = slice(hh * LANES, (hh + 1) * LANES)
        q = qs_ref[pl.ds(r, R), cols]
        k = k_ref[pl.ds(r, R), cols]
        v = v_ref[pl.ds(r, R), cols].astype(bf16)
        hi, mid, lo = _split3(lf_ref[pl.ds(r, R), cols] * LOG2_E)
        c3 = _dot(tril_ref[...], jnp.concatenate([hi, mid, lo], axis=1))
        b = c3[:, :LANES] + c3[:, LANES:2 * LANES] + c3[:, 2 * LANES:]
        a = masks_ref[len(levels)] * _dot_nt(q.astype(bf16), k.astype(bf16))
        for li, n in enumerate(levels):
            e = jnp.exp2(-jnp.abs(b - _level_ref_rows(b, n)))
            a = a + masks_ref[li] * _dot_nt((q * e).astype(bf16), (k * e).astype(bf16))
        st = state_t[hh]
        b_last = b[R - 1:R, :]
        qd = (q * jnp.exp2(b)).astype(bf16)
        kd = (k * jnp.exp2(b_last - b)).astype(bf16)
        o_ref[pl.ds(r, R), cols] = _dot(a.astype(bf16), v) + _dot_nt(qd, st.astype(bf16))
        state_t[hh] = st * jnp.exp2(b_last) + _dot_tn(v, kd)

    def tile(i, carry):
        r = pl.multiple_of(i * R, R)
        for hh in range(HGRN_HEADS_PER_STEP):
            tile_head(r, hh)
        return carry

    lax.fori_loop(0, qs_ref.shape[0] // R, tile, 0)

    @pl.when(step == pl.num_programs(2) - 1)
    def _():
        st_ref[0] = state_t[...]


def _hgrn_seq(qs, k, lf, v, B, S):
    R = HGRN_TILE
    G = HGRN_HEADS_PER_STEP
    rb = min(512, S)
    nst = S // rb
    masks = jnp.asarray(_hgrn_masks(R))
    tril = jnp.asarray(np.tril(np.ones((R, R), np.float32)), dtype=bf16)
    row = pl.BlockSpec((rb, G * LANES), lambda b, h, j: (b * nst + j, h))
    o, st_t = pl.pallas_call(
        _hgrn_seq_kernel,
        grid=(B, H_A // G, nst),
        in_specs=[row, row, row, row,
                  pl.BlockSpec(masks.shape, lambda b, h, j: (0, 0, 0)),
                  pl.BlockSpec((R, R), lambda b, h, j: (0, 0))],
        out_specs=[row, pl.BlockSpec((1, G, DV_A, DK_A), lambda b, h, j: (b, h, 0, 0))],
        out_shape=[jax.ShapeDtypeStruct((B * S, D_A), f32),
                   jax.ShapeDtypeStruct((B, H_A, DV_A, DK_A), f32)],
        scratch_shapes=[pltpu.VMEM((G, DV_A, DK_A), f32)],
        compiler_params=_cparams(("parallel", "parallel", "arbitrary")),
        name="hgrn_seq",
    )(qs, k, lf, v, masks, tril)
    return o, jnp.swapaxes(st_t, 2, 3)


def _hgrn_step_kernel(s_ref, q_ref, k_ref, lf_ref, v_ref, o_ref, so_ref):
    for h in range(H_A):
        s_new = jnp.exp(lf_ref[0, h]) * s_ref[0, h] + k_ref[0, h] * v_ref[0, h]
        so_ref[0, h] = s_new
        o_ref[0, h] = jnp.sum(q_ref[0, h] * s_new, axis=0, keepdims=True)


def _hgrn_step(state, qs, k, lf, v):
    B = state.shape[0]
    col = lambda t: t.reshape(B, H_A, DK_A, 1)
    cspec = pl.BlockSpec((1, H_A, DK_A, 1), lambda b: (b, 0, 0, 0))
    rspec = pl.BlockSpec((1, H_A, 1, DV_A), lambda b: (b, 0, 0, 0))
    sspec = pl.BlockSpec((1, H_A, DK_A, DV_A), lambda b: (b, 0, 0, 0))
    o, s_new = pl.pallas_call(
        _hgrn_step_kernel,
        grid=(B,),
        in_specs=[sspec, cspec, cspec, cspec, rspec],
        out_specs=[rspec, sspec],
        out_shape=[jax.ShapeDtypeStruct((B, H_A, 1, DV_A), f32),
                   jax.ShapeDtypeStruct(state.shape, f32)],
        compiler_params=_cparams(("parallel",)),
        name="hgrn_step",
    )(state, col(qs), col(k), col(lf), v.reshape(B, H_A, 1, DV_A))
    return o.reshape(B, D_A), s_new


def _mix_out_kernel(gated, *refs):
    if gated:
        m_ref, gs_ref, gn_ref, x_ref, wo_ref, lg_ref, lbias_ref, wr_ref, br_ref, x1_ref, xg_ref, idx_ref, gate_ref = refs
        m = m_ref[...] * gs_ref[...]
        m = m * lax.rsqrt(jnp.mean(m * m, -1, keepdims=True) + LN_EPS) * gn_ref[...]
        m = m.astype(bf16)
    else:
        m_ref, x_ref, wo_ref, lg_ref, lbias_ref, wr_ref, br_ref, x1_ref, xg_ref, idx_ref, gate_ref = refs
        m = m_ref[...]
    mix = _dot(m, wo_ref[...])
    x1 = _layer_norm_rows(DEEPNORM_ALPHA * x_ref[...] + mix, lg_ref[...], lbias_ref[...])
    x1_ref[...] = x1
    for j in range(ROW_SLAB):
        xg_ref[:, j, :] = x1[:, j * LANES:(j + 1) * LANES]
    logits =jnp.dot(x1, wr_ref[...], preferred_element_type=f32,
                     precision=lax.Precision.HIGHEST) + br_ref[...]
    lane = lax.broadcasted_iota(jnp.int32, logits.shape, 1).astype(f32)
    cur = logits
    idx_out = jnp.zeros_like(logits)
    gate_out = jnp.zeros_like(logits)
    denom = None
    top = None
    for kk in range(TOP_K):
        v = jnp.max(cur, -1, keepdims=True)
        idx = jnp.min(jnp.where(cur == v, lane, float(N_EXPERTS)), -1, keepdims=True)
        top = v if top is None else top
        e = jnp.exp(v - top)
        denom = e if denom is None else denom + e
        idx_out = jnp.where(lane == kk, idx, idx_out)
        gate_out = jnp.where(lane == kk, e, gate_out)
        cur = jnp.where(lane == idx, -jnp.inf, cur)
    idx_ref[...] = idx_out.astype(jnp.int32)
    gate_ref[...] = gate_out / denom


def _mix_out(m, gate, x2d, w_o, ln_g, ln_b, w_router, b_router, row0=0):
    T = m.shape[0]
    tm = min(256, T)
    blk0 = row0 // tm
    row = pl.BlockSpec((tm, D_MODEL), lambda i: (i, 0))
    vec = pl.BlockSpec((1, D_MODEL), lambda i: (0, 0))
    gated = gate is not None
    ins = [m] + ([gate[0], gate[1].reshape(1, -1)] if gated else []) + [
        x2d, w_o, ln_g.reshape(1, -1), ln_b.reshape(1, -1), w_router, b_router.reshape(1, -1)]
    specs = [row] + ([row, vec] if gated else []) + [
        pl.BlockSpec((tm, D_MODEL), lambda i: (i + blk0, 0)),
        pl.BlockSpec((D_MODEL, D_MODEL), lambda i: (0, 0)), vec, vec,
        pl.BlockSpec((D_MODEL, N_EXPERTS), lambda i: (0, 0)),
        pl.BlockSpec((1, N_EXPERTS), lambda i: (0, 0))]
    return pl.pallas_call(
        functools.partial(_mix_out_kernel, gated),
        grid=(T // tm,),
        in_specs=specs,
        out_specs=[row, pl.BlockSpec((tm, ROW_SLAB, LANES), lambda i: (i, 0, 0))]
                  + [pl.BlockSpec((tm, N_EXPERTS), lambda i: (i, 0))] * 2,
        out_shape=[jax.ShapeDtypeStruct((T, D_MODEL), f32),
                   jax.ShapeDtypeStruct((T, ROW_SLAB, LANES), f32),
                   jax.ShapeDtypeStruct((T, N_EXPERTS), jnp.int32),
                   jax.ShapeDtypeStruct((T, N_EXPERTS), f32)],
        compiler_params=_cparams(("parallel",)),
        name="mix_out_gated" if gated else "mix_out",
    )(*ins)


MOE_FF_CHUNK = 512


def _moe_kernel(blk_e_ref, n_used_ref, tok_ref, x_hbm, w1_ref, b1_ref, w2_ref, b2_ref, y_ref,
                w1b, w2b, xbuf, sem):
    i = pl.program_id(0)
    bm = y_ref.shape[0]
    n_used = n_used_ref[0]
    live = i < n_used
    slot = lax.rem(i, 2)

    def row_copy(src_row, dst_row, s):
        return pltpu.make_async_copy(x_hbm.at[pl.ds(src_row, ROW_SLAB)],
                                     xbuf.at[s, pl.ds(dst_row, ROW_SLAB)], sem.at[s])

    def start_rows(blk, s, r0, n):
        def body(r, carry):
            tok = tok_ref[blk * bm + r]
            row_copy(pl.multiple_of(tok * ROW_SLAB, ROW_SLAB), pl.multiple_of(r * ROW_SLAB, ROW_SLAB), s).start()
            return carry
        lax.fori_loop(r0, r0 + n, body, 0, unroll=8)

    @pl.when(live & (i == 0))
    def _():
        start_rows(0, 0, 0, bm)

    @pl.when(i + 1 < n_used)
    def _():
        start_rows(i + 1, 1 - slot, 0, bm)

    prev = blk_e_ref[jnp.maximum(i - 1, 0)]

    @pl.when(live & ((i == 0) | (blk_e_ref[i] != prev)))
    def _():
        w1b[...] = w1_ref[0, 0].astype(bf16)
        w2b[...] = w2_ref[0, 0].astype(bf16)

    @pl.when(live)
    def _():
        pltpu.make_async_copy(x_hbm.at[pl.ds(0, bm * ROW_SLAB)], xbuf.at[slot], sem.at[slot]).wait()
        x = jnp.concatenate([xbuf[slot, pl.ds(j, bm, stride=ROW_SLAB), :] for j in range(ROW_SLAB)],
                            axis=1).astype(bf16)
        y = None
        for c in range(0, D_FF, MOE_FF_CHUNK):
            hg = _dot(x, w1b[:, c:c + MOE_FF_CHUNK]) + b1_ref[0, 0, :, c:c + MOE_FF_CHUNK]
            hl = (_dot(x, w1b[:, D_FF + c:D_FF + c + MOE_FF_CHUNK])
                  + b1_ref[0, 0, :, D_FF + c:D_FF + c + MOE_FF_CHUNK])
            x_glu = jnp.minimum(hg, SWIGLU_LIMIT)
            x_lin = jnp.clip(hl, -SWIGLU_LIMIT, SWIGLU_LIMIT)
            act = x_glu * _sigmoid(SWIGLU_ALPHA * x_glu) * (x_lin + 1.0)
            part = _dot(act.astype(bf16), w2b[c:c + MOE_FF_CHUNK, :])
            y = part if y is None else y + part
        y_ref[...] = (y + b2_ref[0, 0]).astype(y_ref.dtype)

    @pl.when(jnp.logical_not(live))
    def _():
        y_ref[...] = jnp.zeros_like(y_ref)


def _moe_rows(x_slabs, row_tok, blk_e, n_used, layer, w1, b1, w2, b2, bm):
    nb = row_tok.shape[0] // bm
    grid_spec = pltpu.PrefetchScalarGridSpec(
        num_scalar_prefetch=3,
        grid=(nb,),
        in_specs=[pl.BlockSpec(memory_space=pl.ANY),
                  pl.BlockSpec((1, 1, D_MODEL, 2 * D_FF), lambda i, e, n, t: (layer, e[i], 0, 0)),
                  pl.BlockSpec((1, 1, 1, 2 * D_FF), lambda i, e, n, t: (layer, e[i], 0, 0)),
                  pl.BlockSpec((1, 1, D_FF, D_MODEL), lambda i, e, n, t: (layer, e[i], 0, 0)),
                  pl.BlockSpec((1, 1, 1, D_MODEL), lambda i, e, n, t: (layer, e[i], 0, 0))],
        out_specs=pl.BlockSpec((bm, D_MODEL), lambda i, e, n, t: (i, 0)),
        scratch_shapes=[pltpu.VMEM((D_MODEL, 2 * D_FF), bf16), pltpu.VMEM((D_FF, D_MODEL), bf16),
                        pltpu.VMEM((2, bm * ROW_SLAB, LANES), f32), pltpu.SemaphoreType.DMA((2,))],
    )
    return pl.pallas_call(
        _moe_kernel,
        grid_spec=grid_spec,
        out_shape=jax.ShapeDtypeStruct((nb * bm, D_MODEL), bf16),
        compiler_params=pltpu.CompilerParams(dimension_semantics=("arbitrary",),
                                             vmem_limit_bytes=MOE_VMEM_LIMIT),
        name="moe_experts",
    )(blk_e, n_used, row_tok, x_slabs, w1, b1.reshape(DEPTH, N_EXPERTS, 1, -1), w2,
      b2.reshape(DEPTH, N_EXPERTS, 1, -1))


def _moe_ffn(x_slabs, route_idx, route_gate, layer, w1, b1, w2, b2, bm):
    T = x_slabs.shape[0]
    n_slots = T * TOP_K
    top_idx = route_idx[:, :TOP_K]
    gates = route_gate[:, :TOP_K]
    flat_e = top_idx.reshape(-1)
    order = jnp.argsort(flat_e)
    inv = jnp.argsort(order)
    counts = jnp.bincount(flat_e, length=N_EXPERTS)
    padded = (counts + bm - 1) // bm * bm
    pad_end = jnp.cumsum(padded)
    pad_start = pad_end - padded
    start = jnp.cumsum(counts) - counts
    n_blocks = (n_slots + N_EXPERTS * (bm - 1) + bm - 1) // bm
    blk_e = jnp.minimum(jnp.sum(pad_end[None, :] <= (jnp.arange(n_blocks) * bm)[:, None], axis=1),
                        N_EXPERTS - 1).astype(jnp.int32)
    n_used = (pad_end[-1] // bm).astype(jnp.int32).reshape(1)
    blk = jnp.arange(n_blocks)
    first = start[blk_e] + blk * bm - pad_start[blk_e]
    last = jnp.where(blk < n_used[0], start[blk_e] + counts[blk_e], 0)
    src = first[:, None] + jnp.arange(bm)[None, :]
    row_tok = jnp.where(src < last[:, None], order[jnp.clip(src, 0, n_slots - 1)] // TOP_K, 0)
    row_tok = row_tok.reshape(-1).astype(jnp.int32)
    y_rows = _moe_rows(x_slabs.reshape(T * ROW_SLAB, LANES), row_tok, blk_e, n_used, layer, w1, b1, w2, b2, bm)
    pos = ((pad_start - start)[flat_e] + inv).astype(jnp.int32)
    pos_k = pos.reshape(T, TOP_K).T.reshape(-1)
    return y_rows[pos_k].reshape(TOP_K, T, D_MODEL), gates


def _ffn_combine(x_ref, y_refs, gt_ref, g_ref, b_ref):
    gt = gt_ref[...]
    y = y_refs[0][0].astype(f32) * gt[:, 0:1]
    for kk in range(1, TOP_K):
        y = y + y_refs[kk][0].astype(f32) * gt[:, kk:kk + 1]
    return _layer_norm_rows(DEEPNORM_ALPHA * x_ref[...] + y, g_ref[...], b_ref[...])


def _ffn_out_kernel(n_proj, x_ref, y0, y1, y2, y3, gt_ref, g_ref, b_ref, *rest):
    x2 = _ffn_combine(x_ref, (y0, y1, y2, y3), gt_ref, g_ref, b_ref)
    rest[n_proj][...] = x2
    xb = x2.astype(bf16)
    for p in range(n_proj):
        rest[n_proj + 1 + p][...] = _dot(xb, rest[p][...])


def _ffn_in_specs(tm, row_index):
    row = pl.BlockSpec((tm, D_MODEL), lambda *g: (row_index(*g), 0))
    yk = [pl.BlockSpec((1, tm, D_MODEL), functools.partial(lambda kk, *g: (kk, row_index(*g), 0), kk))
          for kk in range(TOP_K)]
    vec = pl.BlockSpec((1, D_MODEL), lambda *g: (0, 0))
    return row, [row] + yk + [pl.BlockSpec((tm, TOP_K), lambda *g: (row_index(*g), 0)), vec, vec]


def _ffn_out(x1, yk, gates, ln_g, ln_b, proj_ws):
    T = x1.shape[0]
    tm = min(256, T)
    row, specs = _ffn_in_specs(tm, lambda i: i)
    n_proj = len(proj_ws)
    return pl.pallas_call(
        functools.partial(_ffn_out_kernel, n_proj),
        grid=(T // tm,),
        in_specs=specs + [pl.BlockSpec(w.shape, lambda i: (0, 0)) for w in proj_ws],
        out_specs=[row] + [pl.BlockSpec((tm, w.shape[1]), lambda i: (i, 0)) for w in proj_ws],
        out_shape=[jax.ShapeDtypeStruct((T, D_MODEL), f32)] +
                  [jax.ShapeDtypeStruct((T, w.shape[1]), f32) for w in proj_ws],
        compiler_params=_cparams(("parallel",)),
        name="ffn_out_%d" % n_proj,
    )(x1, yk, yk, yk, yk, gates, ln_g.reshape(1, -1), ln_b.reshape(1, -1), *proj_ws)


def _ffn_out_seq_kernel(x_ref, y0, y1, y2, y3, gt_ref, g_ref, b_ref, wk_ref, wkt_ref, wv_ref, wvt_ref, wqt_ref,
                        x2_ref, kt_ref, v3_ref, kb_ref, vt_ref, qt_ref):
    x2 = _ffn_combine(x_ref, (y0, y1, y2, y3), gt_ref, g_ref, b_ref)
    x2_ref[...] = x2
    xb = x2.astype(bf16)
    kb_ref[...] = _dot(xb, wk_ref[...]).astype(bf16)
    kt_ref[0] = _dot_nt(wkt_ref[...], xb)
    vt_ref[0] = _dot_nt(wvt_ref[...], xb).astype(bf16)
    qt_ref[0] = _dot_nt(wqt_ref[...], xb).astype(bf16)
    v = _dot(xb, wv_ref[...])
    for h in range(H_B):
        v3_ref[:, h, :] = v[:, h * LANES:(h + 1) * LANES]


def _ffn_out_seq(x1, yk, gates, ln_g, ln_b, W, B, S):
    T = B * S
    tm = 256
    nj = S // tm
    row, specs = _ffn_in_specs(tm, lambda b, j: b * nj + j)
    wspec = pl.BlockSpec((D_MODEL, D_MODEL), lambda b, j: (0, 0), pipeline_mode=pl.Buffered(1))
    tspec = pl.BlockSpec((1, D_MODEL, tm), lambda b, j: (b, 0, j))
    return pl.pallas_call(
        _ffn_out_seq_kernel,
        grid=(B, nj),
        in_specs=specs + [wspec] * 5,
        out_specs=[row, tspec, pl.BlockSpec((tm, H_B, LANES), lambda b, j: (b * nj + j, 0, 0)), row, tspec, tspec],
        out_shape=[jax.ShapeDtypeStruct((T, D_MODEL), f32),
                   jax.ShapeDtypeStruct((B, D_MODEL, S), f32),
                   jax.ShapeDtypeStruct((T, H_B, LANES), f32),
                   jax.ShapeDtypeStruct((T, D_MODEL), bf16),
                   jax.ShapeDtypeStruct((B, D_MODEL, S), bf16),
                   jax.ShapeDtypeStruct((B, D_MODEL, S), bf16)],
        compiler_params=_cparams(("parallel", "parallel")),
        name="ffn_out_seq",
    )(x1, yk, yk, yk, yk, gates, ln_g.reshape(1, -1), ln_b.reshape(1, -1),
      W['w_k'], W['w_k'].T, W['w_v'], W['w_v'].T, W['w_q_scaled'].T)


ATT_T = 512


def _attn_seq_kernel(lam_init, slopes_ref, lam_ref, qt_ref, k_ref, vt_ref, g_ref, o_ref,
                     qt2_scr, bias_scr, bias_diag_scr, acc_scr):
    t = ATT_T
    h = pl.program_id(1)
    i = pl.program_id(2)
    slope = slopes_ref[h]

    @pl.when(i == 0)
    def _():
        kk = lax.broadcasted_iota(jnp.int32, (t, 2 * t), 0)
        qq = lax.broadcasted_iota(jnp.int32, (t, 2 * t), 1)
        qq = jnp.where(qq >= t, qq - t, qq)
        rel = slope * (kk - qq).astype(f32)
        bias_scr[...] = rel
        bias_diag_scr[...] = jnp.where(kk <= qq, rel, NEG_INF)

    qt = qt_ref[0]
    dim = lax.broadcasted_iota(jnp.int32, qt.shape, 0)
    zero = jnp.zeros_like(qt)
    qt2_scr[:, :t] = jnp.where(dim < HD_B, qt, zero)
    qt2_scr[:, t:] = jnp.where(dim >= HD_B, qt, zero)
    acc_scr[...] = jnp.zeros_like(acc_scr)

    def kv_block(j, bias_ref, m_old, l_old):
        c = pl.multiple_of(j * t, t)
        s = _dot(k_ref[pl.ds(c, t), :], qt2_scr[...]) + bias_ref[...]
        shift = slope * ((i - j) * t).astype(f32)
        m_new = jnp.maximum(m_old, jnp.max(s, 0, keepdims=True) - shift)
        p = jnp.exp(s - (m_new + shift))
        alpha = jnp.exp(m_old - m_new)
        l_new = alpha * l_old + jnp.sum(p, 0, keepdims=True)
        acc_scr[...] = alpha * acc_scr[...] + _dot(vt_ref[0, :, pl.ds(c, t)], p.astype(bf16))
        return m_new, l_new

    m0 = jnp.full((1, 2 * t), NEG_INF, f32)
    l0 = jnp.zeros((1, 2 * t), f32)
    m, l = lax.fori_loop(0, i, lambda j, ml: kv_block(j, bias_scr, *ml), (m0, l0))
    m, l = kv_block(i, bias_diag_scr, m, l)

    o = acc_scr[...] / l
    d = o[:, :t] - lam_ref[0] * o[:, t:]
    d = d * lax.rsqrt(jnp.mean(d * d, 0, keepdims=True) + LN_EPS) * g_ref[...]
    o_ref[...] = (d * (1.0 - lam_init)).T.astype(o_ref.dtype)


def _attn_seq(qt, kb, vt, lam, subln, lam_init, B, S):
    t = ATT_T
    nq = S // t
    slopes = jnp.exp2(-8.0 * jnp.arange(1, H_B + 1, dtype=f32) / H_B)
    smem = pl.BlockSpec(memory_space=pltpu.SMEM)
    return pl.pallas_call(
        functools.partial(_attn_seq_kernel, lam_init),
        grid=(B, H_B, nq),
        in_specs=[smem, smem,
                  pl.BlockSpec((1, LANES, t), lambda b, h, i: (b, h, i)),
                  pl.BlockSpec((S, LANES), lambda b, h, i: (b, h)),
                  pl.BlockSpec((1, LANES, S), lambda b, h, i: (b, h, 0)),
                  pl.BlockSpec((LANES, 1), lambda b, h, i: (0, 0))],
        out_specs=pl.BlockSpec((t, LANES), lambda b, h, i: (b * nq + i, h)),
        out_shape=jax.ShapeDtypeStruct((B * S, D_MODEL), bf16),
        scratch_shapes=[pltpu.VMEM((LANES, 2 * t), bf16),
                        pltpu.VMEM((t, 2 * t), f32),
                        pltpu.VMEM((t, 2 * t), f32),
                        pltpu.VMEM((LANES, 2 * t), f32)],
        compiler_params=_cparams(("parallel", "arbitrary", "arbitrary")),
        name="attn_seq",
    )(slopes, lam.reshape(1), qt, kb, vt, subln.reshape(-1, 1))


PAGES_PER_STEP = 8


def _attn_paged_kernel(lam_init, n_pages, pt_ref, slopes_ref, lam_ref, q_ref, kn_ref, vn_ref, g_ref, *rest):
    G = PAGES_PER_STEP
    k_refs = rest[:G]
    v_refs = rest[G:2 * G]
    o_ref, qbd_scr, m_scr, l_scr, acc_scr = rest[2 * G:]
    step = pl.program_id(1)
    nrow = 2 * H_B
    r1 = lax.broadcasted_iota(jnp.int32, (nrow, 1), 0)
    slope_col = jnp.zeros((nrow, 1), f32)
    for hh in range(H_B):
        slope_col = jnp.where(r1 // 2 == hh, slopes_ref[hh], slope_col)

    @pl.when(step == 0)
    def _():
        rowi = lax.broadcasted_iota(jnp.int32, (nrow, D_MODEL), 0)
        coli = lax.broadcasted_iota(jnp.int32, (nrow, D_MODEL), 1)
        own = (coli // HD_B) == rowi
        q = q_ref[0] * (HD_B ** -0.5)
        qbd = jnp.where(own, jnp.broadcast_to(q, (nrow, D_MODEL)), 0.0).astype(bf16)
        qbd_scr[...] = qbd
        m_scr[...] = jnp.sum(qbd.astype(f32) * kn_ref[0].astype(bf16).astype(f32), -1, keepdims=True)
        l_scr[...] = jnp.ones_like(l_scr)
        acc_scr[...] = jnp.broadcast_to(vn_ref[0].astype(bf16).astype(f32), (nrow, D_MODEL))

    s = jnp.concatenate([_dot(qbd_scr[...], k_refs[u][0].astype(bf16)) for u in range(G)], axis=1)
    tok = lax.broadcasted_iota(jnp.int32, s.shape, 1)
    dist = ((n_pages - step * G) * PAGE_SIZE - tok).astype(f32)
    s = s - slope_col * dist
    m = m_scr[...]
    m_new = jnp.maximum(m, jnp.max(s, -1, keepdims=True))
    p32 = jnp.exp(s - m_new)
    alpha = jnp.exp(m - m_new)
    l = alpha * l_scr[...] + jnp.sum(p32, -1, keepdims=True)
    p = p32.astype(bf16)
    acc = alpha * acc_scr[...]
    for u in range(G):
        vb = jnp.concatenate([v_refs[u][0, pl.ds(hh, PAGE_SIZE, stride=H_B), :] for hh in range(H_B)],
                             axis=1).astype(bf16)
        acc = acc + _dot(p[:, u * PAGE_SIZE:(u + 1) * PAGE_SIZE], vb)
    m_scr[...] = m_new
    l_scr[...] = l
    acc_scr[...] = acc

    @pl.when(step == pl.num_programs(1) - 1)
    def _():
        o = acc / l
        for hh in range(H_B):
            blk = o[2 * hh:2 * hh + 2, hh * LANES:(hh + 1) * LANES]
            d = blk[0:1, :] - lam_ref[0] * blk[1:2, :]
            d = d * lax.rsqrt(jnp.mean(d * d, -1, keepdims=True) + LN_EPS) * g_ref[...]
            o_ref[0, :, hh * LANES:(hh + 1) * LANES] = (d * (1.0 - lam_init)).astype(o_ref.dtype)


def _attn_paged(q, k_new, v_new, cache_kt, cache_v, page_table, lam, subln, lam_init):
    B = q.shape[0]
    G = PAGES_PER_STEP
    n_pages = page_table.shape[1]
    slopes = jnp.exp2(-8.0 * jnp.arange(1, H_B + 1, dtype=f32) / H_B)
    smem = pl.BlockSpec(memory_space=pltpu.SMEM)
    vec = pl.BlockSpec((1, 1, D_MODEL), lambda b, s, pt: (b, 0, 0))

    def page_index(u, ndim):
        return lambda b, s, pt: (pt[b * n_pages + s * G + u],) + (0,) * (ndim - 1)

    k_specs = [pl.BlockSpec((1, D_MODEL, PAGE_SIZE), page_index(u, 3)) for u in range(G)]
    v_specs = [pl.BlockSpec((1, PAGE_SIZE * H_B, LANES), page_index(u, 3)) for u in range(G)]
    grid_spec = pltpu.PrefetchScalarGridSpec(
        num_scalar_prefetch=1,
        grid=(B, n_pages // G),
        in_specs=[smem, smem, vec, vec, vec, pl.BlockSpec((1, LANES), lambda b, s, pt: (0, 0))] + k_specs + v_specs,
        out_specs=vec,
        scratch_shapes=[pltpu.VMEM((2 * H_B, D_MODEL), bf16),
                        pltpu.VMEM((2 * H_B, 1), f32),
                        pltpu.VMEM((2 * H_B, 1), f32),
                        pltpu.VMEM((2 * H_B, D_MODEL), f32)],
    )
    d = pl.pallas_call(
        functools.partial(_attn_paged_kernel, lam_init, n_pages),
        grid_spec=grid_spec,
        out_shape=jax.ShapeDtypeStruct((B, 1, D_MODEL), bf16),
        compiler_params=_cparams(("parallel", "arbitrary")),
        name="attn_paged",
    )(page_table.reshape(-1), slopes, lam.reshape(1), q.reshape(B, 1, -1), k_new.reshape(B, 1, -1),
      v_new.reshape(B, 1, -1), subln.reshape(1, -1), *([cache_kt] * G),
      *([cache_v.reshape(-1, PAGE_SIZE * H_B, LANES)] * G))
    return d.reshape(B, D_MODEL)


def _mix0(o, gs, x2d, W, row0):
    return _mix_out(o, (gs, W['gnorm']), x2d, W['w_o_a'], W['ln_mix_g'][0], W['ln_mix_b'][0],
                    W['w_router'][0], W['b_router'][0], row0)


def _mix1(d, x2, W):
    return _mix_out(d, None, x2, W['w_o_b'], W['ln_mix_g'][1], W['ln_mix_b'][1],
                    W['w_router'][1], W['b_router'][1])


def _moe(x, route_idx, route_gate, layer, W, moe_rows):
    return _moe_ffn(x, route_idx, route_gate, layer, W['w1'], W['b1'], W['w2'], W['b2'], moe_rows)


PROMPT_CHUNKS = 2


def _trunk_prompt(x, W, lam, subln, lam_init):
    B, S, _ = x.shape
    bc = B // PROMPT_CHUNKS
    tc = bc * S
    cs = range(PROMPT_CHUNKS)
    x2d = x.reshape(B * S, D_MODEL)
    pj = [_inproj(x2d, W['w_in'], W['lb'], c * tc, tc) for c in cs]
    hg = [_hgrn_seq(*pj[c][:4], bc, S) for c in cs]
    m0 = [_mix0(hg[c][0], pj[c][4], x2d, W, c * tc) for c in cs]
    e0 = [_moe(*m0[c][1:], 0, W, MOE_ROWS) for c in cs]
    f0 = [_ffn_out_seq(m0[c][0], e0[c][0], e0[c][1], W['ln_ffn_g'][0], W['ln_ffn_b'][0], W, bc, S) for c in cs]
    at = [_attn_seq(f0[c][5], f0[c][3], f0[c][4], lam, subln, lam_init, bc, S) for c in cs]
    m1 = [_mix1(at[c], f0[c][0], W) for c in cs]
    e1 = [_moe(*m1[c][1:], 1, W, MOE_ROWS) for c in cs]
    x4 = [_ffn_out(m1[c][0], e1[c][0], e1[c][1], W['ln_ffn_g'][1], W['ln_ffn_b'][1], [])[0] for c in cs]
    y = jnp.concatenate(x4, axis=0).reshape(B, S, D_MODEL)
    state = jnp.concatenate([hg[c][1] for c in cs], axis=0)
    kt = jnp.concatenate([f0[c][1] for c in cs], axis=0)
    v3 = jnp.concatenate([f0[c][2] for c in cs], axis=0)
    k_out = jnp.transpose(kt.reshape(B, H_B, 2, HD_B, S), (0, 4, 1, 2, 3))
    return y, state[None], k_out, v3.reshape(B, S, H_B, 2 * HD_B)


def _trunk_sample(x, state0, cache_kt, cache_v, page_table, W, lam, subln, lam_init):
    B = x.shape[0]
    x2d = x.reshape(B, D_MODEL)
    qs, k, lf, v, gs = _inproj(x2d, W['w_in'], W['lb'], 0, B)
    o, state = _hgrn_step(state0, qs, k, lf, v)
    x1, x1g, r_idx, r_gate = _mix0(o, gs, x2d, W, 0)
    yk, gates = _moe(x1g, r_idx, r_gate, 0, W, 32)
    x2, k_sh, v_sh, q = _ffn_out(x1, yk, gates, W['ln_ffn_g'][0], W['ln_ffn_b'][0],
                                 [W['w_k'], W['w_v'], W['w_q']])
    d = _attn_paged(q, k_sh, v_sh, cache_kt, cache_v, page_table, lam, subln, lam_init)
    x3, x3g, r_idx, r_gate = _mix1(d, x2, W)
    yk, gates = _moe(x3g, r_idx, r_gate, 1, W, 32)
    (x4,) = _ffn_out(x3, yk, gates, W['ln_ffn_g'][1], W['ln_ffn_b'][1], [])
    return (x4.reshape(B, 1, D_MODEL), state[None],
            k_sh.reshape(B, 1, H_B, 2, HD_B), v_sh.reshape(B, 1, H_B, 2 * HD_B))


def kernel(x_prompt, x_sample, state_hgrn, cache_k, cache_v, page_table, w_in_a, lb_logits, gnorm_a, w_o_a,
           w_kv, w_q_b, lam_q1, lam_k1, lam_q2, lam_k2, subln_b, w_o_b, ln_mix_g, ln_mix_b, ln_ffn_g, ln_ffn_b,
           w_router, b_router, w1, b1, w2, b2):
    q_dim = H_B * 2 * HD_B
    lb = jnp.cumsum(jax.nn.softmax(lb_logits.astype(f32), axis=0), axis=0)[:N_A_LAYERS]
    W = dict(
        lb=lb[0].reshape(1, D_A), w_in=w_in_a[0].astype(bf16), gnorm=gnorm_a[0], w_o_a=w_o_a[0].astype(bf16),
        w_k=w_kv[:, :q_dim].astype(bf16), w_v=w_kv[:, q_dim:].astype(bf16), w_q=w_q_b[0].astype(bf16),
        w_q_scaled=(w_q_b[0] * (HD_B ** -0.5)).astype(bf16),
        w_o_b=w_o_b[0].astype(bf16), ln_mix_g=ln_mix_g, ln_mix_b=ln_mix_b, ln_ffn_g=ln_ffn_g, ln_ffn_b=ln_ffn_b,
        w_router=w_router, b_router=b_router, w1=w1, b1=b1, w2=w2, b2=b2)
    layer = N_A_LAYERS
    lam_init = 0.8 - 0.6 * math.exp(-0.3 * layer)
    lam = (jnp.exp(jnp.sum((lam_q1[0] * lam_k1[0]).astype(f32)))
           - jnp.exp(jnp.sum((lam_q2[0] * lam_k2[0]).astype(f32))) + lam_init)
    subln = subln_b[0]

    y_p, st_p, k_p, v_p = _trunk_prompt(x_prompt, W, lam, subln, lam_init)

    n_pool = cache_k.shape[0]
    cache_kt = jnp.transpose(cache_k, (0, 2, 3, 4, 1)).reshape(n_pool, D_MODEL, PAGE_SIZE)
    y_s, st_s, k_s, v_s = _trunk_sample(x_sample, state_hgrn[0], cache_kt, cache_v, page_table, W,
                                        lam, subln, lam_init)
    return (y_p, y_s, st_p, st_s, k_p, v_p, k_s, v_s)
```
